```python
import jax, jax.numpy as jnp
from jax import lax
import numpy as np

D_MODEL = 1024
BATCH = 8
SEQ = 2048
DEPTH = 2

N_MIXERS = 2
N_CONV_LAYERS = (DEPTH + 1) // 2
N_POOL_LAYERS = DEPTH // 2
CONV_WIDTH = 3
POOL_WINDOWS = (2, 4, 8, 16)
N_POOL_GROUPS = len(POOL_WINDOWS)
POOL_GROUP_W = D_MODEL // N_POOL_GROUPS
N_KEYS = 128
N_EXPERTS = N_KEYS * N_KEYS
PEER_HEADS = 8
PEER_TOPK = 16
D_QUERY = 256
D_HALF = D_QUERY // 2
TOK_CHUNK = 128
N_MOD = 6
EPS = 1e-6

kernel_name = "hybrid_shortconv_pool_peer_adaln"


def rmsnorm(x, g):
    xf = x.astype(jnp.float32)
    y = xf * lax.rsqrt(jnp.mean(xf * xf, axis=-1, keepdims=True) + EPS)
    return (y * g.astype(jnp.float32)).astype(x.dtype)


def modulate(h, shift, scale):
    return h * (1.0 + scale[:, None, :]) + shift[:, None, :]


def short_conv_mixer(h, w_in, conv_w, w_out):
    S = h.shape[1]
    proj = h @ w_in
    b_gate, c_gate, u = jnp.split(proj, 3, axis=-1)
    v = c_gate * u
    vp = jnp.pad(v, ((0, 0), (CONV_WIDTH - 1, 0), (0, 0)))
    conv = conv_w[0] * vp[:, 0:S]
    for k in range(1, CONV_WIDTH):
        conv = conv + conv_w[k] * vp[:, k:k + S]
    return (b_gate * conv) @ w_out


def pool_mixer(h, w_in, w_grp, scale, w_out):
    B, S, D = h.shape
    u = (h @ w_in).reshape(B, S, N_POOL_GROUPS, POOL_GROUP_W)
    cs = jnp.cumsum(u.astype(jnp.float32), axis=1)
    t = jnp.arange(S)
    pooled = []
    for g, w in enumerate(POOL_WINDOWS):
        cg = cs[:, :, g]
        lag = jnp.pad(cg, ((0, 0), (w, 0), (0, 0)))[:, :S]
        cnt = jnp.minimum(t + 1, w).astype(jnp.float32)[None, :, None]
        pooled.append((cg - lag) / cnt)
    pooled = jnp.stack(pooled, axis=2).astype(u.dtype) - u
    z = jnp.einsum('bsgi,gio->bsgo', pooled, w_grp).reshape(B, S, D) * scale
    return z @ w_out


def peer_ffn(h, w_q, k1, k2, u_tab, v_tab):
    B, S, D = h.shape
    T = B * S
    ht = h.reshape(T, D)
    q = (ht @ w_q).reshape(T, PEER_HEADS, 2, D_HALF)
    s1 = jnp.einsum('thd,nd->thn', q[:, :, 0], k1)
    s2 = jnp.einsum('thd,nd->thn', q[:, :, 1], k2)
    v1, i1 = lax.top_k(s1, PEER_TOPK)
    v2, i2 = lax.top_k(s2, PEER_TOPK)
    cand = (v1[..., :, None] + v2[..., None, :]).reshape(T, PEER_HEADS, PEER_TOPK * PEER_TOPK)
    sc, flat = lax.top_k(cand, PEER_TOPK)
    e1 = jnp.take_along_axis(i1, flat // PEER_TOPK, axis=-1)
    e2 = jnp.take_along_axis(i2, flat % PEER_TOPK, axis=-1)
    idx = e1 * N_KEYS + e2
    gate = jax.nn.softmax(sc.astype(jnp.float32), axis=-1).astype(h.dtype)
    n_chunks = T // TOK_CHUNK

    def block(args):
        xc, ic, gc = args
        ug = u_tab[ic]
        act = jax.nn.gelu(jnp.einsum('td,thkd->thk', xc, ug), approximate=False)
        vg = v_tab[ic]
        return jnp.einsum('thk,thkd->td', gc * act, vg)

    out = lax.map(block, (ht.reshape(n_chunks, TOK_CHUNK, D),
                          idx.reshape(n_chunks, TOK_CHUNK, PEER_HEADS, PEER_TOPK),
                          gate.reshape(n_chunks, TOK_CHUNK, PEER_HEADS, PEER_TOPK)))
    return out.reshape(B, S, D)


def setup_inputs(seed: int = 0) -> dict:
    key = jax.random.key(seed)
    ks = jax.random.split(key, 20)
    D = D_MODEL
    nrm = lambda k, shape, s: jax.random.normal(k, shape, jnp.float32) * s
    inv = D ** -0.5
    return {
        "x": nrm(ks[0], (BATCH, SEQ, D), 1.0),
        "c": nrm(ks[1], (BATCH, D), 1.0),
        "norm1_g": 1.0 + nrm(ks[2], (DEPTH, D), 0.05),
        "norm2_g": 1.0 + nrm(ks[3], (DEPTH, D), 0.05),
        "w_mod": nrm(ks[4], (DEPTH, D, N_MOD * D), 0.5 * inv),
        "b_mod": nrm(ks[5], (DEPTH, N_MOD * D), 0.02),
        "conv_w_in": nrm(ks[6], (N_CONV_LAYERS, D, 3 * D), inv),
        "conv_w": nrm(ks[7], (N_CONV_LAYERS, CONV_WIDTH, D), CONV_WIDTH ** -0.5),
        "conv_w_out": nrm(ks[8], (N_CONV_LAYERS, D, D), inv),
        "pool_w_in": nrm(ks[9], (N_POOL_LAYERS, D, D), inv),
        "pool_w_grp": nrm(ks[10], (N_POOL_LAYERS, N_POOL_GROUPS, POOL_GROUP_W, POOL_GROUP_W), POOL_GROUP_W ** -0.5),
        "pool_scale": 1.0 + nrm(ks[11], (N_POOL_LAYERS, D), 0.1),
        "pool_w_out": nrm(ks[12], (N_POOL_LAYERS, D, D), inv),
        "peer_w_q": nrm(ks[13], (DEPTH, D, PEER_HEADS * D_QUERY), inv),
        "peer_k1": nrm(ks[14], (DEPTH, N_KEYS, D_HALF), D_HALF ** -0.5),
        "peer_k2": nrm(ks[15], (DEPTH, N_KEYS, D_HALF), D_HALF ** -0.5),
        "peer_u": nrm(ks[16], (DEPTH, N_EXPERTS, D), inv),
        "peer_v": nrm(ks[17], (DEPTH, N_EXPERTS, D), 0.5),
        "final_g": 1.0 + nrm(ks[18], (D,), 0.05),
    }


def reference(x, c, norm1_g, norm2_g, w_mod, b_mod, conv_w_in, conv_w, conv_w_out,
              pool_w_in, pool_w_grp, pool_scale, pool_w_out,
              peer_w_q, peer_k1, peer_k2, peer_u, peer_v, final_g):
    c_act = jax.nn.silu(c)
    for i in range(DEPTH):
        mod = c_act @ w_mod[i] + b_mod[i]
        sh1, sc1, g1, sh2, sc2, g2 = jnp.split(mod, N_MOD, axis=-1)
        h = modulate(rmsnorm(x, norm1_g[i]), sh1, sc1)
        j = i // N_MIXERS
        if i % N_MIXERS == 0:
            y = short_conv_mixer(h, conv_w_in[j], conv_w[j], conv_w_out[j])
        else:
            y = pool_mixer(h, pool_w_in[j], pool_w_grp[j], pool_scale[j], pool_w_out[j])
        x = x + g1[:, None, :] * y
        h = modulate(rmsnorm(x, norm2_g[i]), sh2, sc2)
        x = x + g2[:, None, :] * peer_ffn(h, peer_w_q[i], peer_k1[i], peer_k2[i], peer_u[i], peer_v[i])
    return rmsnorm(x, final_g)
```

```python
import functools
import math

import jax
import jax.numpy as jnp
from jax import lax
from jax.experimental import pallas as pl
from jax.experimental.pallas import tpu as pltpu

D_MODEL = 1024
BATCH = 8
SEQ = 2048
DEPTH = 2
CONV_WIDTH = 3
POOL_WINDOWS = (2, 4, 8, 16)
N_POOL_GROUPS = len(POOL_WINDOWS)
POOL_GROUP_W = D_MODEL // N_POOL_GROUPS
N_KEYS = 128
N_EXPERTS = N_KEYS * N_KEYS
PEER_HEADS = 8
PEER_TOPK = 16
D_QUERY = 256
D_HALF = D_QUERY // 2
N_MOD = 6
EPS = 1e-6
N_TOKENS = BATCH * SEQ

V7X_VMEM_BYTES = 64 * 1024 * 1024
LANES = 128
SUBLANES = 8

MIX_TOKENS = 512
ROUTER_TOKENS = 256
EXPERT_TOKENS = 512
EXPERT_CHUNK = 1024
E1_PER_CHUNK = EXPERT_CHUNK // N_KEYS
POOL_HALO = 16

F32 = jnp.float32
BF16 = jnp.bfloat16


def _vmem_limit(nbytes):
    return int(min(nbytes, V7X_VMEM_BYTES - 8 * 1024 * 1024))


def _norm_mod(x, g, scale, shift):
    ms = jnp.mean(x * x, axis=-1, keepdims=True)
    y = x * lax.rsqrt(ms + EPS)
    return (y * g) * (1.0 + scale) + shift


def _gelu(x):
    return 0.5 * x * (1.0 + lax.erf(x * (1.0 / math.sqrt(2.0))))


def _mod_kernel(c_ref, w_ref, b_ref, out_ref):
    c = c_ref[...]
    c_act = c * jax.nn.sigmoid(c)
    out_ref[...] = jnp.dot(c_act, w_ref[...], preferred_element_type=F32,
                           precision=lax.Precision.HIGHEST) + b_ref[...]


def _mod_call(c, w_mod, b_mod):
    d = D_MODEL
    return pl.pallas_call(
        _mod_kernel,
        grid=(DEPTH, N_MOD),
        in_specs=[
            pl.BlockSpec((BATCH, d), lambda l, n: (0, 0)),
            pl.BlockSpec((None, d, d), lambda l, n: (l, 0, n)),
            pl.BlockSpec((None, 1, d), lambda l, n: (l, 0, n)),
        ],
        out_specs=pl.BlockSpec((None, BATCH, d), lambda l, n: (l, 0, n)),
        out_shape=jax.ShapeDtypeStruct((DEPTH, BATCH, N_MOD * d), F32),
        name="adaln_mod",
    )(c, w_mod, b_mod.reshape(DEPTH, 1, N_MOD * d))


def _conv_kernel(x_ref, mod_ref, g_ref, w_in_ref, cw_ref, w_out_ref, out_ref, tail_ref):
    d = D_MODEL
    ts = MIX_TOKENS

    @pl.when(pl.program_id(1) == 0)
    def _():
        tail_ref[...] = jnp.zeros_like(tail_ref)

    x = x_ref[...]
    mod = mod_ref[...]
    h = _norm_mod(x, g_ref[...], mod[:, d:2 * d], mod[:, 0:d]).astype(BF16)
    proj = jnp.dot(h, w_in_ref[...], preferred_element_type=F32)
    b_gate = proj[:, 0:d]
    v = proj[:, d:2 * d] * proj[:, 2 * d:3 * d]
    tail = tail_ref[...]
    p1 = tail[SUBLANES - 1:SUBLANES]
    p2 = tail[SUBLANES - 2:SUBLANES - 1]
    rows = lax.broadcasted_iota(jnp.int32, (ts, d), 0)
    vm1 = jnp.where(rows >= 1, pltpu.roll(v, 1, 0), p1)
    vm2 = jnp.where(rows >= 2, pltpu.roll(v, 2, 0), jnp.where(rows == 1, p1, p2))
    cw = cw_ref[...]
    conv = cw[0:1] * vm2 + cw[1:2] * vm1 + cw[2:3] * v
    tail_ref[...] = v[ts - SUBLANES:ts]
    y = jnp.dot((b_gate * conv).astype(BF16), w_out_ref[...], preferred_element_type=F32)
    out_ref[...] = x + mod[:, 2 * d:3 * d] * y


def _conv_call(x, mod_l, g, w_in, cw, w_out):
    d = D_MODEL
    ts = MIX_TOKENS
    return pl.pallas_call(
        _conv_kernel,
        grid=(BATCH, SEQ // ts),
        in_specs=[
            pl.BlockSpec((None, ts, d), lambda b, s: (b, s, 0)),
            pl.BlockSpec((None, 1, N_MOD * d), lambda b, s: (b, 0, 0)),
            pl.BlockSpec((1, d), lambda b, s: (0, 0)),
            pl.BlockSpec((d, 3 * d), lambda b, s: (0, 0)),
            pl.BlockSpec((CONV_WIDTH, d), lambda b, s: (0, 0)),
            pl.BlockSpec((d, d), lambda b, s: (0, 0)),
        ],
        out_specs=pl.BlockSpec((None, ts, d), lambda b, s: (b, s, 0)),
        out_shape=jax.ShapeDtypeStruct((BATCH, SEQ, d), F32),
        scratch_shapes=[pltpu.VMEM((SUBLANES, d), F32)],
        compiler_params=pltpu.CompilerParams(
            dimension_semantics=("arbitrary", "arbitrary"),
            vmem_limit_bytes=_vmem_limit(48 * 1024 * 1024)),
        name="conv_mixer",
    )(x, mod_l, g, w_in, cw, w_out)


def _pool_kernel(x_ref, mod_ref, g_ref, w_in_ref, w_grp_ref, scale_ref, w_out_ref, out_ref, halo_ref):
    d = D_MODEL
    ts = MIX_TOKENS
    gw = POOL_GROUP_W
    s_blk = pl.program_id(1)

    @pl.when(s_blk == 0)
    def _():
        halo_ref[...] = jnp.zeros_like(halo_ref)

    x = x_ref[...]
    mod = mod_ref[...]
    h = _norm_mod(x, g_ref[...], mod[:, d:2 * d], mod[:, 0:d]).astype(BF16)
    u = jnp.dot(h, w_in_ref[...], preferred_element_type=F32)
    ext = jnp.concatenate([halo_ref[...], u], axis=0)
    halo_ref[...] = u[ts - POOL_HALO:ts]
    t_pos = s_blk * ts + lax.broadcasted_iota(jnp.int32, (ts, gw), 0)
    zs = []
    for gi, win in enumerate(POOL_WINDOWS):
        acc = ext[:, gi * gw:(gi + 1) * gw]
        span = 1
        while span < win:
            acc = acc + pltpu.roll(acc, span, 0)
            span *= 2
        wsum = acc[POOL_HALO:POOL_HALO + ts]
        cnt = jnp.minimum(t_pos + 1, win).astype(F32)
        pooled = wsum / cnt - u[:, gi * gw:(gi + 1) * gw]
        zs.append(jnp.dot(pooled.astype(BF16), w_grp_ref[gi], preferred_element_type=F32))
    z = jnp.concatenate(zs, axis=1) * scale_ref[...]
    y = jnp.dot(z.astype(BF16), w_out_ref[...], preferred_element_type=F32)
    out_ref[...] = x + mod[:, 2 * d:3 * d] * y


def _pool_call(x, mod_l, g, w_in, w_grp, scale, w_out):
    d = D_MODEL
    ts = MIX_TOKENS
    gw = POOL_GROUP_W
    return pl.pallas_call(
        _pool_kernel,
        grid=(BATCH, SEQ // ts),
        in_specs=[
            pl.BlockSpec((None, ts, d), lambda b, s: (b, s, 0)),
            pl.BlockSpec((None, 1, N_MOD * d), lambda b, s: (b, 0, 0)),
            pl.BlockSpec((1, d), lambda b, s: (0, 0)),
            pl.BlockSpec((d, d), lambda b, s: (0, 0)),
            pl.BlockSpec((N_POOL_GROUPS, gw, gw), lambda b, s: (0, 0, 0)),
            pl.BlockSpec((1, d), lambda b, s: (0, 0)),
            pl.BlockSpec((d, d), lambda b, s: (0, 0)),
        ],
        out_specs=pl.BlockSpec((None, ts, d), lambda b, s: (b, s, 0)),
        out_shape=jax.ShapeDtypeStruct((BATCH, SEQ, d), F32),
        scratch_shapes=[pltpu.VMEM((POOL_HALO, d), F32)],
        compiler_params=pltpu.CompilerParams(
            dimension_semantics=("arbitrary", "arbitrary"),
            vmem_limit_bytes=_vmem_limit(48 * 1024 * 1024)),
        name="pool_mixer",
    )(x, mod_l, g, w_in, w_grp, scale, w_out)


def _top_values(s, k):
    rows = []
    for r in range(k):
        m = jnp.max(s, axis=0, keepdims=True)
        rows.append(m)
        if r + 1 < k:
            s = jnp.where(s >= m, -jnp.inf, s)
    return jnp.concatenate(rows, axis=0)


def _router_kernel(x_ref, mod_ref, g_ref, wq_ref, k1_ref, k2_ref,
                   ht_ref, a_ref, ea_ref, b_ref, eb_ref, thr_ref, q_ref):
    d = D_MODEL
    mod = mod_ref[...]
    h = _norm_mod(x_ref[...], g_ref[...], mod[:, 4 * d:5 * d], mod[:, 3 * d:4 * d])
    ht = h.T.astype(BF16)
    ht_ref[...] = ht
    q_ref[...] = jnp.dot(wq_ref[...], ht, preferred_element_type=F32)

    def head(hd, carry):
        row = pl.multiple_of(hd * D_QUERY, D_QUERY)
        s1 = jnp.dot(k1_ref[...], q_ref[pl.ds(row, D_HALF), :],
                     preferred_element_type=F32, precision=lax.Precision.HIGHEST)
        s2 = jnp.dot(k2_ref[...], q_ref[pl.ds(row + D_HALF, D_HALF), :],
                     preferred_element_type=F32, precision=lax.Precision.HIGHEST)
        v1 = _top_values(s1, PEER_TOPK)
        v2 = _top_values(s2, PEER_TOPK)
        cand = jnp.concatenate([v1[r:r + 1] + v2 for r in range(PEER_TOPK)], axis=0)
        top = _top_values(cand, PEER_TOPK)
        z = jnp.sum(jnp.exp(top - top[0:1]), axis=0, keepdims=True)
        a_ref[hd] = s1
        b_ref[hd] = s2
        ea_ref[hd] = jnp.exp(s1 - v1[0:1]) / z
        eb_ref[hd] = jnp.exp(s2 - v2[0:1])
        thr_ref[pl.ds(hd, 1), :] = top[PEER_TOPK - 1:PEER_TOPK]
        return carry

    lax.fori_loop(0, PEER_HEADS, head, 0)


def _router_call(x_flat, mod_l, g, wq_t, k1, k2):
    d = D_MODEL
    tr = ROUTER_TOKENS
    nt = N_TOKENS
    key_shape = jax.ShapeDtypeStruct((PEER_HEADS, N_KEYS, nt), F32)
    key_spec = pl.BlockSpec((PEER_HEADS, N_KEYS, tr), lambda i: (0, 0, i))
    return pl.pallas_call(
        _router_kernel,
        grid=(nt // tr,),
        in_specs=[
            pl.BlockSpec((tr, d), lambda i: (i, 0)),
            pl.BlockSpec((None, 1, N_MOD * d), lambda i: (i * tr // SEQ, 0, 0)),
            pl.BlockSpec((1, d), lambda i: (0, 0)),
            pl.BlockSpec((PEER_HEADS * D_QUERY, d), lambda i: (0, 0)),
            pl.BlockSpec((N_KEYS, D_HALF), lambda i: (0, 0)),
            pl.BlockSpec((N_KEYS, D_HALF), lambda i: (0, 0)),
        ],
        out_specs=[
            pl.BlockSpec((d, tr), lambda i: (0, i)),
            key_spec, key_spec, key_spec, key_spec,
            pl.BlockSpec((PEER_HEADS, tr), lambda i: (0, i)),
        ],
        out_shape=[
            jax.ShapeDtypeStruct((d, nt), BF16),
            key_shape, key_shape, key_shape, key_shape,
            jax.ShapeDtypeStruct((PEER_HEADS, nt), F32),
        ],
        scratch_shapes=[pltpu.VMEM((PEER_HEADS * D_QUERY, tr), F32)],
        compiler_params=pltpu.CompilerParams(
            dimension_semantics=("arbitrary",),
            vmem_limit_bytes=_vmem_limit(48 * 1024 * 1024)),
        name="peer_router",
    )(x_flat, mod_l, g, wq_t, k1, k2)


def _expert_kernel(ht_ref, u_ref, vt_ref, a_ref, ea_ref, b_ref, eb_ref, thr_ref,
                   x_ref, mod_ref, fg_ref, out_ref, acc_ref, act_ref, w_ref, *, final_norm):
    d = D_MODEL
    j = pl.program_id(1)

    @pl.when(j == 0)
    def _():
        acc_ref[...] = jnp.zeros_like(acc_ref)

    act_ref[...] = jnp.dot(u_ref[...], ht_ref[...], preferred_element_type=F32)
    for e in range(E1_PER_CHUNK):
        er = slice(e * N_KEYS, (e + 1) * N_KEYS)
        for c in range(EXPERT_TOKENS // LANES):
            tc = slice(c * LANES, (c + 1) * LANES)
            gate = jnp.zeros((N_KEYS, LANES), F32)
            for hd in range(PEER_HEADS):
                s = a_ref[hd, e:e + 1, tc] + b_ref[hd, :, tc]
                gate = jnp.where(s >= thr_ref[hd:hd + 1, tc],
                                 gate + ea_ref[hd, e:e + 1, tc] * eb_ref[hd, :, tc], gate)
            w_ref[er, tc] = (_gelu(act_ref[er, tc]) * gate).astype(BF16)
    acc_ref[...] += jnp.dot(vt_ref[...], w_ref[...], preferred_element_type=F32)

    @pl.when(j == pl.num_programs(1) - 1)
    def _():
        mod = mod_ref[...]
        xn = x_ref[...] + mod[:, 5 * d:6 * d] * acc_ref[...].T
        if final_norm:
            ms = jnp.mean(xn * xn, axis=-1, keepdims=True)
            xn = xn * lax.rsqrt(ms + EPS) * fg_ref[...]
        out_ref[...] = xn


def _expert_call(ht, u_tab, vt_tab, a, ea, b, eb, thr, x_flat, mod_l, final_g, final_norm):
    d = D_MODEL
    tb = EXPERT_TOKENS
    ec = EXPERT_CHUNK
    nt = N_TOKENS
    row_spec = pl.BlockSpec((PEER_HEADS, E1_PER_CHUNK, tb), lambda i, j: (0, j, i))
    col_spec = pl.BlockSpec((PEER_HEADS, N_KEYS, tb), lambda i, j: (0, 0, i))
    return pl.pallas_call(
        functools.partial(_expert_kernel, final_norm=final_norm),
        grid=(nt // tb, N_EXPERTS // ec),
        in_specs=[
            pl.BlockSpec((d, tb), lambda i, j: (0, i)),
            pl.BlockSpec((ec, d), lambda i, j: (j, 0)),
            pl.BlockSpec((d, ec), lambda i, j: (0, j)),
            row_spec, row_spec, col_spec, col_spec,
            pl.BlockSpec((PEER_HEADS, tb), lambda i, j: (0, i)),
            pl.BlockSpec((tb, d), lambda i, j: (i, 0)),
            pl.BlockSpec((None, 1, N_MOD * d), lambda i, j: (i * tb // SEQ, 0, 0)),
            pl.BlockSpec((1, d), lambda i, j: (0, 0)),
        ],
        out_specs=pl.BlockSpec((tb, d), lambda i, j: (i, 0)),
        out_shape=jax.ShapeDtypeStruct((nt, d), F32),
        scratch_shapes=[
            pltpu.VMEM((d, tb), F32),
            pltpu.VMEM((ec, tb), F32),
            pltpu.VMEM((ec, tb), BF16),
        ],
        compiler_params=pltpu.CompilerParams(
            dimension_semantics=("arbitrary", "arbitrary"),
            vmem_limit_bytes=_vmem_limit(52 * 1024 * 1024)),
        name="peer_experts",
    )(ht, u_tab, vt_tab, a, ea, b, eb, thr, x_flat, mod_l, final_g)


def kernel(x, c, norm1_g, norm2_g, w_mod, b_mod, conv_w_in, conv_w, conv_w_out, pool_w_in, pool_w_grp,
           pool_scale, pool_w_out, peer_w_q, peer_k1, peer_k2, peer_u, peer_v, final_g):
    d = D_MODEL
    mod = _mod_call(c, w_mod, b_mod).reshape(DEPTH, BATCH, 1, N_MOD * d)
    fg = final_g.reshape(1, d)
    for i in range(DEPTH):
        g1 = norm1_g[i].reshape(1, d)
        g2 = norm2_g[i].reshape(1, d)
        jm = i // 2
        if i % 2 == 0:
            x = _conv_call(x, mod[i], g1, conv_w_in[jm].astype(BF16), conv_w[jm],
                           conv_w_out[jm].astype(BF16))
        else:
            x = _pool_call(x, mod[i], g1, pool_w_in[jm].astype(BF16), pool_w_grp[jm].astype(BF16),
                           pool_scale[jm].reshape(1, d), pool_w_out[jm].astype(BF16))
        x_flat = x.reshape(N_TOKENS, d)
        ht, a, ea, b, eb, thr = _router_call(x_flat, mod[i], g2, peer_w_q[i].T.astype(BF16),
                                             peer_k1[i], peer_k2[i])
        x_flat = _expert_call(ht, peer_u[i].astype(BF16), peer_v[i].T.astype(BF16), a, ea, b, eb, thr,
                              x_flat, mod[i], fg, final_norm=(i == DEPTH - 1))
        x = x_flat.reshape(BATCH, SEQ, d)
    return x
```

```python
import functools
import math

import jax
import jax.numpy as jnp
from jax import lax
from jax.experimental import pallas as pl
from jax.experimental.pallas import tpu as pltpu

D_MODEL = 1024
BATCH = 8
SEQ = 2048
DEPTH = 2
CONV_WIDTH = 3
POOL_WINDOWS = (2, 4, 8, 16)
N_POOL_GROUPS = len(POOL_WINDOWS)
POOL_GROUP_W = D_MODEL // N_POOL_GROUPS
N_KEYS = 128
N_EXPERTS = N_KEYS * N_KEYS
PEER_HEADS = 8
PEER_TOPK = 16
D_QUERY = 256
D_HALF = D_QUERY // 2
N_MOD = 6
EPS = 1e-6
N_TOKENS = BATCH * SEQ

V7X_VMEM_BYTES = 64 * 1024 * 1024
LANES = 128
SUBLANES = 8
BF16_ROWS = 2 * SUBLANES

MIX_TOKENS = 512
ROUTER_TOKENS = 256
EXPERT_TOKENS = 512
EXPERT_CHUNK = 1024
E1_PER_CHUNK = EXPERT_CHUNK // N_KEYS
POOL_HALO = 16

F32 = jnp.float32
BF16 = jnp.bfloat16
U32 = jnp.uint32


def _vmem_limit(nbytes):
    return int(min(nbytes, V7X_VMEM_BYTES - 8 * 1024 * 1024))


def _norm_mod(x, g, scale, shift):
    ms = jnp.mean(x * x, axis=-1, keepdims=True)
    y = x * lax.rsqrt(ms + EPS)
    return (y * g) * (1.0 + scale) + shift


def _gelu(x):
    return 0.5 * x * (1.0 + lax.erf(x * (1.0 / math.sqrt(2.0))))


PACK_ROWS = 1024


def _pack_rows_kernel(x_ref, o_ref):
    o_ref[...] = pltpu.bitcast(x_ref[...].astype(BF16), U32)


def _pack_rows_call(x):
    n, m = x.shape
    return pl.pallas_call(
        _pack_rows_kernel,
        grid=(n // PACK_ROWS,),
        in_specs=[pl.BlockSpec((PACK_ROWS, m), lambda i: (i, 0))],
        out_specs=pl.BlockSpec((PACK_ROWS // 2, m), lambda i: (i, 0)),
        out_shape=jax.ShapeDtypeStruct((n // 2, m), U32),
        name="pack_rows",
    )(x)


def _pack_cols_kernel(x_ref, o_ref):
    o_ref[...] = pltpu.bitcast(x_ref[...].T.astype(BF16), U32)


def _pack_cols_call(x):
    n, m = x.shape
    return pl.pallas_call(
        _pack_cols_kernel,
        grid=(n // PACK_ROWS,),
        in_specs=[pl.BlockSpec((PACK_ROWS, m), lambda i: (i, 0))],
        out_specs=pl.BlockSpec((m // 2, PACK_ROWS), lambda i: (0, i)),
        out_shape=jax.ShapeDtypeStruct((m // 2, n), U32),
        name="pack_cols",
    )(x)


def _mod_kernel(c_ref, w_ref, b_ref, out_ref):
    c = c_ref[...]
    c_act = c * jax.nn.sigmoid(c)
    out_ref[...] = jnp.dot(c_act, w_ref[...], preferred_element_type=F32,
                           precision=lax.Precision.HIGHEST) + b_ref[...]


def _mod_call(c, w_mod, b_mod):
    d = D_MODEL
    return pl.pallas_call(
        _mod_kernel,
        grid=(DEPTH, N_MOD),
        in_specs=[
            pl.BlockSpec((BATCH, d), lambda l, n: (0, 0)),
            pl.BlockSpec((None, d, d), lambda l, n: (l, 0, n)),
            pl.BlockSpec((None, 1, d), lambda l, n: (l, 0, n)),
        ],
        out_specs=pl.BlockSpec((None, BATCH, d), lambda l, n: (l, 0, n)),
        out_shape=jax.ShapeDtypeStruct((DEPTH, BATCH, N_MOD * d), F32),
        name="adaln_mod",
    )(c, w_mod, b_mod.reshape(DEPTH, 1, N_MOD * d))


def _conv_kernel(x_ref, mod_ref, g_ref, w_in_ref, cw_ref, w_out_ref, out_ref, tail_ref):
    d = D_MODEL
    ts = MIX_TOKENS

    @pl.when(pl.program_id(1) == 0)
    def _():
        tail_ref[...] = jnp.zeros_like(tail_ref)

    x = x_ref[...]
    mod = mod_ref[...]
    h = _norm_mod(x, g_ref[...], mod[:, d:2 * d], mod[:, 0:d]).astype(BF16)
    proj = jnp.dot(h, w_in_ref[...], preferred_element_type=F32)
    b_gate = proj[:, 0:d]
    v = proj[:, d:2 * d] * proj[:, 2 * d:3 * d]
    tail = tail_ref[...]
    p1 = tail[SUBLANES - 1:SUBLANES]
    p2 = tail[SUBLANES - 2:SUBLANES - 1]
    rows = lax.broadcasted_iota(jnp.int32, (ts, d), 0)
    vm1 = jnp.where(rows >= 1, pltpu.roll(v, 1, 0), p1)
    vm2 = jnp.where(rows >= 2, pltpu.roll(v, 2, 0), jnp.where(rows == 1, p1, p2))
    cw = cw_ref[...]
    conv = cw[0:1] * vm2 + cw[1:2] * vm1 + cw[2:3] * v
    tail_ref[...] = v[ts - SUBLANES:ts]
    y = jnp.dot((b_gate * conv).astype(BF16), w_out_ref[...], preferred_element_type=F32)
    out_ref[...] = x + mod[:, 2 * d:3 * d] * y


def _conv_call(x, mod_l, g, w_in, cw, w_out):
    d = D_MODEL
    ts = MIX_TOKENS
    return pl.pallas_call(
        _conv_kernel,
        grid=(BATCH, SEQ // ts),
        in_specs=[
            pl.BlockSpec((None, ts, d), lambda b, s: (b, s, 0)),
            pl.BlockSpec((None, 1, N_MOD * d), lambda b, s: (b, 0, 0)),
            pl.BlockSpec((1, d), lambda b, s: (0, 0)),
            pl.BlockSpec((d, 3 * d), lambda b, s: (0, 0)),
            pl.BlockSpec((CONV_WIDTH, d), lambda b, s: (0, 0)),
            pl.BlockSpec((d, d), lambda b, s: (0, 0)),
        ],
        out_specs=pl.BlockSpec((None, ts, d), lambda b, s: (b, s, 0)),
        out_shape=jax.ShapeDtypeStruct((BATCH, SEQ, d), F32),
        scratch_shapes=[pltpu.VMEM((SUBLANES, d), F32)],
        compiler_params=pltpu.CompilerParams(
            dimension_semantics=("arbitrary", "arbitrary"),
            vmem_limit_bytes=_vmem_limit(48 * 1024 * 1024)),
        name="conv_mixer",
    )(x, mod_l, g, w_in, cw, w_out)


def _pool_kernel(x_ref, mod_ref, g_ref, w_in_ref, w_grp_ref, scale_ref, w_out_ref, out_ref, halo_ref):
    d = D_MODEL
    ts = MIX_TOKENS
    gw = POOL_GROUP_W
    s_blk = pl.program_id(1)

    @pl.when(s_blk == 0)
    def _():
        halo_ref[...] = jnp.zeros_like(halo_ref)

    x = x_ref[...]
    mod = mod_ref[...]
    h = _norm_mod(x, g_ref[...], mod[:, d:2 * d], mod[:, 0:d]).astype(BF16)
    u = jnp.dot(h, w_in_ref[...], preferred_element_type=F32)
    ext = jnp.concatenate([halo_ref[...], u], axis=0)
    halo_ref[...] = u[ts - POOL_HALO:ts]
    t_pos = s_blk * ts + lax.broadcasted_iota(jnp.int32, (ts, gw), 0)
    zs = []
    for gi, win in enumerate(POOL_WINDOWS):
        acc = ext[:, gi * gw:(gi + 1) * gw]
        span = 1
        while span < win:
            acc = acc + pltpu.roll(acc, span, 0)
            span *= 2
        wsum = acc[POOL_HALO:POOL_HALO + ts]
        cnt = jnp.minimum(t_pos + 1, win).astype(F32)
        pooled = wsum / cnt - u[:, gi * gw:(gi + 1) * gw]
        zs.append(jnp.dot(pooled.astype(BF16), w_grp_ref[gi], preferred_element_type=F32))
    z = jnp.concatenate(zs, axis=1) * scale_ref[...]
    y = jnp.dot(z.astype(BF16), w_out_ref[...], preferred_element_type=F32)
    out_ref[...] = x + mod[:, 2 * d:3 * d] * y


def _pool_call(x, mod_l, g, w_in, w_grp, scale, w_out):
    d = D_MODEL
    ts = MIX_TOKENS
    gw = POOL_GROUP_W
    return pl.pallas_call(
        _pool_kernel,
        grid=(BATCH, SEQ // ts),
        in_specs=[
            pl.BlockSpec((None, ts, d), lambda b, s: (b, s, 0)),
            pl.BlockSpec((None, 1, N_MOD * d), lambda b, s: (b, 0, 0)),
            pl.BlockSpec((1, d), lambda b, s: (0, 0)),
            pl.BlockSpec((d, d), lambda b, s: (0, 0)),
            pl.BlockSpec((N_POOL_GROUPS, gw, gw), lambda b, s: (0, 0, 0)),
            pl.BlockSpec((1, d), lambda b, s: (0, 0)),
            pl.BlockSpec((d, d), lambda b, s: (0, 0)),
        ],
        out_specs=pl.BlockSpec((None, ts, d), lambda b, s: (b, s, 0)),
        out_shape=jax.ShapeDtypeStruct((BATCH, SEQ, d), F32),
        scratch_shapes=[pltpu.VMEM((POOL_HALO, d), F32)],
        compiler_params=pltpu.CompilerParams(
            dimension_semantics=("arbitrary", "arbitrary"),
            vmem_limit_bytes=_vmem_limit(48 * 1024 * 1024)),
        name="pool_mixer",
    )(x, mod_l, g, w_in, w_grp, scale, w_out)


def _top_values(s, k, with_ranks=False):
    rows = []
    rank = jnp.full(s.shape, float(k), F32)
    for r in range(k):
        m = jnp.max(s, axis=0, keepdims=True)
        rows.append(m)
        hit = s >= m
        if with_ranks:
            rank = jnp.where(hit, float(r), rank)
        if r + 1 < k:
            s = jnp.where(hit, -jnp.inf, s)
    vals = jnp.concatenate(rows, axis=0)
    return (vals, rank) if with_ranks else vals


def _pair_word(x):
    bits = lax.bitcast_convert_type(x.astype(BF16).astype(F32), U32)
    return bits | (bits >> 16)


def _router_kernel(x_ref, mod_ref, g_ref, wq_ref, k1_ref, k2_ref,
                   ht_ref, nw_ref, eaw_ref, rb_ref, eb_ref, q_ref):
    d = D_MODEL
    k = PEER_TOPK
    mod = mod_ref[...]
    h = _norm_mod(x_ref[...], g_ref[...], mod[:, 4 * d:5 * d], mod[:, 3 * d:4 * d])
    ht = h.T.astype(BF16)
    ht_ref[...] = pltpu.bitcast(ht, U32)
    q_ref[...] = jnp.dot(pltpu.bitcast(wq_ref[...], BF16), ht, preferred_element_type=F32)

    def head(hd, carry):
        row = pl.multiple_of(hd * D_QUERY, D_QUERY)
        s1 = jnp.dot(k1_ref[...], q_ref[pl.ds(row, D_HALF), :],
                     preferred_element_type=F32, precision=lax.Precision.HIGHEST)
        s2 = jnp.dot(k2_ref[...], q_ref[pl.ds(row + D_HALF, D_HALF), :],
                     preferred_element_type=F32, precision=lax.Precision.HIGHEST)
        v1 = _top_values(s1, k)
        v2, rank2 = _top_values(s2, k, with_ranks=True)
        pieces = [v1[r:r + 1] + v2[0:k // (r + 1)] for r in range(k)]
        n_cand = sum(p.shape[0] for p in pieces)
        pad = -n_cand % SUBLANES
        if pad:
            pieces.append(jnp.full((pad, v1.shape[1]), -jnp.inf, F32))
        top = _top_values(jnp.concatenate(pieces, axis=0), k)
        thr = top[k - 1:k]
        z = jnp.sum(jnp.exp(top - top[0:1]), axis=0, keepdims=True)
        n_sorted = jnp.zeros_like(v1)
        for r2 in range(k):
            n_sorted = jnp.where(v1 + v2[r2:r2 + 1] >= thr, float(r2 + 1), n_sorted)
        n1 = jnp.zeros_like(s1)
        for r1 in range(k):
            n1 = jnp.where(s1 == v1[r1:r1 + 1], n_sorted[r1:r1 + 1], n1)
        nw_ref[hd] = _pair_word(n1)
        eaw_ref[hd] = _pair_word(jnp.exp(s1 - v1[0:1]) / z)
        rb_ref[hd] = pltpu.bitcast(rank2.astype(BF16), U32)
        eb_ref[hd] = pltpu.bitcast(jnp.exp(s2 - v2[0:1]).astype(BF16), U32)
        return carry

    lax.fori_loop(0, PEER_HEADS, head, 0)


def _router_call(x_flat, mod_l, g, wq_t, k1, k2):
    d = D_MODEL
    tr = ROUTER_TOKENS
    nt = N_TOKENS
    key_spec = pl.BlockSpec((PEER_HEADS, N_KEYS, tr), lambda i: (0, 0, i))
    pair_spec = pl.BlockSpec((PEER_HEADS, N_KEYS // 2, tr), lambda i: (0, 0, i))
    return pl.pallas_call(
        _router_kernel,
        grid=(nt // tr,),
        in_specs=[
            pl.BlockSpec((tr, d), lambda i: (i, 0)),
            pl.BlockSpec((None, 1, N_MOD * d), lambda i: (i * tr // SEQ, 0, 0)),
            pl.BlockSpec((1, d), lambda i: (0, 0)),
            pl.BlockSpec((PEER_HEADS * D_QUERY // 2, d), lambda i: (0, 0)),
            pl.BlockSpec((N_KEYS, D_HALF), lambda i: (0, 0)),
            pl.BlockSpec((N_KEYS, D_HALF), lambda i: (0, 0)),
        ],
        out_specs=[
            pl.BlockSpec((d // 2, tr), lambda i: (0, i)),
            key_spec, key_spec, pair_spec, pair_spec,
        ],
        out_shape=[
            jax.ShapeDtypeStruct((d // 2, nt), U32),
            jax.ShapeDtypeStruct((PEER_HEADS, N_KEYS, nt), U32),
            jax.ShapeDtypeStruct((PEER_HEADS, N_KEYS, nt), U32),
            jax.ShapeDtypeStruct((PEER_HEADS, N_KEYS // 2, nt), U32),
            jax.ShapeDtypeStruct((PEER_HEADS, N_KEYS // 2, nt), U32),
        ],
        scratch_shapes=[pltpu.VMEM((PEER_HEADS * D_QUERY, tr), F32)],
        compiler_params=pltpu.CompilerParams(
            dimension_semantics=("arbitrary",),
            vmem_limit_bytes=_vmem_limit(48 * 1024 * 1024)),
        name="peer_router",
    )(x_flat, mod_l, g, wq_t, k1, k2)


def _gate_pair(e, act_ref, w_ref, nw_ref, eaw_ref, rb_ref, eb_ref):
    n_sub = N_KEYS // BF16_ROWS
    for c in range(EXPERT_TOKENS // LANES):
        tc = slice(c * LANES, (c + 1) * LANES)
        gates = [[jnp.zeros((BF16_ROWS, LANES), BF16) for _ in range(n_sub)] for _ in range(2)]
        for hd in range(PEER_HEADS):
            n_rows, ea_rows = [], []
            for de in range(2):
                nw = jnp.broadcast_to(nw_ref[hd, e + de:e + de + 1, tc], (SUBLANES, LANES))
                ew = jnp.broadcast_to(eaw_ref[hd, e + de:e + de + 1, tc], (SUBLANES, LANES))
                n_rows.append(pltpu.bitcast(nw, BF16))
                ea_rows.append(pltpu.bitcast(ew, BF16))
            for kk in range(n_sub):
                ws = slice(kk * SUBLANES, (kk + 1) * SUBLANES)
                rank2 = pltpu.bitcast(rb_ref[hd, ws, tc], BF16)
                eb = pltpu.bitcast(eb_ref[hd, ws, tc], BF16)
                for de in range(2):
                    g = gates[de][kk]
                    gates[de][kk] = jnp.where(rank2 < n_rows[de], g + ea_rows[de] * eb, g)
        for de in range(2):
            for kk in range(n_sub):
                rows = slice((e + de) * N_KEYS + kk * BF16_ROWS, (e + de) * N_KEYS + (kk + 1) * BF16_ROWS)
                w_ref[rows, tc] = _gelu(act_ref[rows, tc]).astype(BF16) * gates[de][kk]


def _expert_kernel(ht_ref, u_ref, vt_ref, nw_ref, eaw_ref, rb_ref, eb_ref, x_ref, mod_ref, fg_ref,
                   out_ref, acc_ref, act0_ref, act1_ref, w0_ref, w1_ref, *, final_norm, n_steps, n_chunks):
    d = D_MODEL
    s = pl.program_id(0)
    sc = jnp.clip(s - 2, 0, n_steps - 1)
    jc = sc % n_chunks

    @pl.when(s == 0)
    def _():
        act1_ref[...] = jnp.zeros_like(act1_ref)
        w0_ref[...] = jnp.zeros_like(w0_ref)

    @pl.when(jc == 0)
    def _():
        acc_ref[...] = jnp.zeros_like(acc_ref)

    def step(act_a, act_b, w_b, w_c):
        n_pairs = E1_PER_CHUNK // 2
        rows_a = EXPERT_CHUNK // n_pairs
        rows_c = d // n_pairs
        for p in range(n_pairs):
            ra = slice(p * rows_a, (p + 1) * rows_a)
            rc = slice(p * rows_c, (p + 1) * rows_c)
            u = pltpu.bitcast(u_ref[p * rows_a // 2:(p + 1) * rows_a // 2, :], BF16)
            vt = pltpu.bitcast(vt_ref[p * rows_c // 2:(p + 1) * rows_c // 2, :], BF16)
            act_a[ra, :] = jnp.dot(u, pltpu.bitcast(ht_ref[...], BF16), preferred_element_type=F32)
            _gate_pair(2 * p, act_b, w_b, nw_ref, eaw_ref, rb_ref, eb_ref)
            acc_ref[rc, :] += jnp.dot(vt, w_c[...], preferred_element_type=F32)

    @pl.when(s % 2 == 0)
    def _():
        step(act0_ref, act1_ref, w1_ref, w0_ref)

    @pl.when(s % 2 == 1)
    def _():
        step(act1_ref, act0_ref, w0_ref, w1_ref)

    @pl.when((jc == n_chunks - 1) & (s >= 2))
    def _():
        mod = mod_ref[...]
        xn = x_ref[...] + mod[:, 5 * d:6 * d] * acc_ref[...].T
        if final_norm:
            ms = jnp.mean(xn * xn, axis=-1, keepdims=True)
            xn = xn * lax.rsqrt(ms + EPS) * fg_ref[...]
        out_ref[...] = xn


def _expert_call(ht, u_tab, vt_tab, nw, eaw, rb, eb, x_flat, mod_l, final_g, final_norm):
    d = D_MODEL
    tb = EXPERT_TOKENS
    ec = EXPERT_CHUNK
    nt = N_TOKENS
    n_chunks = N_EXPERTS // ec
    n_steps = (nt // tb) * n_chunks

    def stage(lag):
        def split(s):
            t = jnp.clip(s - lag, 0, n_steps - 1)
            return t // n_chunks, t % n_chunks
        return split

    sa, sb, sc = stage(0), stage(1), stage(2)
    row_spec = pl.BlockSpec((PEER_HEADS, E1_PER_CHUNK, tb), lambda s: (0, sb(s)[1], sb(s)[0]))
    col_spec = pl.BlockSpec((PEER_HEADS, N_KEYS // 2, tb), lambda s: (0, 0, sb(s)[0]))
    return pl.pallas_call(
        functools.partial(_expert_kernel, final_norm=final_norm, n_steps=n_steps, n_chunks=n_chunks),
        grid=(n_steps + 2,),
        in_specs=[
            pl.BlockSpec((d // 2, tb), lambda s: (0, sa(s)[0])),
            pl.BlockSpec((ec // 2, d), lambda s: (sa(s)[1], 0)),
            pl.BlockSpec((d // 2, ec), lambda s: (0, sc(s)[1])),
            row_spec, row_spec, col_spec, col_spec,
            pl.BlockSpec((tb, d), lambda s: (sc(s)[0], 0)),
            pl.BlockSpec((None, 1, N_MOD * d), lambda s: (sc(s)[0] * tb // SEQ, 0, 0)),
            pl.BlockSpec((1, d), lambda s: (0, 0)),
        ],
        out_specs=pl.BlockSpec((tb, d), lambda s: (sc(s)[0], 0)),
        out_shape=jax.ShapeDtypeStruct((nt, d), F32),
        scratch_shapes=[
            pltpu.VMEM((d, tb), F32),
            pltpu.VMEM((ec, tb), F32),
            pltpu.VMEM((ec, tb), F32),
            pltpu.VMEM((ec, tb), BF16),
            pltpu.VMEM((ec, tb), BF16),
        ],
        compiler_params=pltpu.CompilerParams(
            dimension_semantics=("arbitrary",),
            vmem_limit_bytes=_vmem_limit(52 * 1024 * 1024)),
        name="peer_experts",
    )(ht, u_tab, vt_tab, nw, eaw, rb, eb, x_flat, mod_l, final_g)


def kernel(x, c, norm1_g, norm2_g, w_mod, b_mod, conv_w_in, conv_w, conv_w_out, pool_w_in, pool_w_grp,
           pool_scale, pool_w_out, peer_w_q, peer_k1, peer_k2, peer_u, peer_v, final_g):
    d = D_MODEL
    mod = _mod_call(c, w_mod, b_mod).reshape(DEPTH, BATCH, 1, N_MOD * d)
    fg = final_g.reshape(1, d)
    for i in range(DEPTH):
        g1 = norm1_g[i].reshape(1, d)
        g2 = norm2_g[i].reshape(1, d)
        jm = i // 2
        if i % 2 == 0:
            x = _conv_call(x, mod[i], g1, conv_w_in[jm].astype(BF16), conv_w[jm],
                           conv_w_out[jm].astype(BF16))
        else:
            x = _pool_call(x, mod[i], g1, pool_w_in[jm].astype(BF16), pool_w_grp[jm].astype(BF16),
                           pool_scale[jm].reshape(1, d), pool_w_out[jm].astype(BF16))
        x_flat = x.reshape(N_TOKENS, d)
        ht, nw, eaw, rb, eb = _router_call(x_flat, mod[i], g2, _pack_cols_call(peer_w_q[i]),
                                           peer_k1[i], peer_k2[i])
        x_flat = _expert_call(ht, _pack_rows_call(peer_u[i]), _pack_cols_call(peer_v[i]), nw, eaw, rb, eb,
                              x_flat, mod[i], fg, final_norm=(i == DEPTH - 1))
        x = x_flat.reshape(BATCH, SEQ, d)
    return x
```

```python
import functools
import math

import jax
import jax.numpy as jnp
from jax import lax
from jax.experimental import pallas as pl
from jax.experimental.pallas import tpu as pltpu

D_MODEL = 1024
BATCH = 8
SEQ = 2048
DEPTH = 2
CONV_WIDTH = 3
POOL_WINDOWS = (2, 4, 8, 16)
N_POOL_GROUPS = len(POOL_WINDOWS)
POOL_GROUP_W = D_MODEL // N_POOL_GROUPS
N_KEYS = 128
N_EXPERTS = N_KEYS * N_KEYS
PEER_HEADS = 8
PEER_TOPK = 16
D_QUERY = 256
D_HALF = D_QUERY // 2
N_MOD = 6
EPS = 1e-6
N_TOKENS = BATCH * SEQ

V7X_VMEM_BYTES = 64 * 1024 * 1024
LANES = 128
SUBLANES = 8
BF16_ROWS = 2 * SUBLANES

MIX_TOKENS = 512
ROUTER_TOKENS = 256
EXPERT_TOKENS = 512
EXPERT_CHUNK = 2048
E1_PER_CHUNK = EXPERT_CHUNK // N_KEYS
MXU_ROWS = 512
POOL_HALO = 16

F32 = jnp.float32
BF16 = jnp.bfloat16
U32 = jnp.uint32


def _vmem_limit(nbytes):
    return int(min(nbytes, V7X_VMEM_BYTES - 8 * 1024 * 1024))


def _norm_mod(x, g, scale, shift):
    ms = jnp.mean(x * x, axis=-1, keepdims=True)
    y = x * lax.rsqrt(ms + EPS)
    return (y * g) * (1.0 + scale) + shift


def _gelu(x):
    return 0.5 * x * (1.0 + lax.erf(x * (1.0 / math.sqrt(2.0))))


PACK_ROWS = 1024


def _pack_rows_kernel(x_ref, o_ref):
    o_ref[...] = pltpu.bitcast(x_ref[...].astype(BF16), U32)


def _pack_rows_call(x, layer):
    _, n, m = x.shape
    return pl.pallas_call(
        _pack_rows_kernel,
        grid=(n // PACK_ROWS,),
        in_specs=[pl.BlockSpec((None, PACK_ROWS, m), lambda i: (layer, i, 0))],
        out_specs=pl.BlockSpec((PACK_ROWS // 2, m), lambda i: (i, 0)),
        out_shape=jax.ShapeDtypeStruct((n // 2, m), U32),
        name="pack_rows",
    )(x)


def _pack_cols_kernel(x_ref, o_ref):
    o_ref[...] = pltpu.bitcast(x_ref[...].T.astype(BF16), U32)


def _pack_cols_call(x, layer):
    _, n, m = x.shape
    return pl.pallas_call(
        _pack_cols_kernel,
        grid=(n // PACK_ROWS,),
        in_specs=[pl.BlockSpec((None, PACK_ROWS, m), lambda i: (layer, i, 0))],
        out_specs=pl.BlockSpec((m // 2, PACK_ROWS), lambda i: (0, i)),
        out_shape=jax.ShapeDtypeStruct((m // 2, n), U32),
        name="pack_cols",
    )(x)


def _mod_kernel(c_ref, w_ref, b_ref, out_ref):
    c = c_ref[...]
    c_act = c * jax.nn.sigmoid(c)
    out_ref[...] = jnp.dot(c_act, w_ref[...], preferred_element_type=F32,
                           precision=lax.Precision.HIGHEST) + b_ref[...]


def _mod_call(c, w_mod, b_mod):
    d = D_MODEL
    return pl.pallas_call(
        _mod_kernel,
        grid=(DEPTH, N_MOD),
        in_specs=[
            pl.BlockSpec((BATCH, d), lambda l, n: (0, 0)),
            pl.BlockSpec((None, d, d), lambda l, n: (l, 0, n)),
            pl.BlockSpec((None, 1, d), lambda l, n: (l, 0, n)),
        ],
        out_specs=pl.BlockSpec((None, BATCH, d), lambda l, n: (l, 0, n)),
        out_shape=jax.ShapeDtypeStruct((DEPTH, BATCH, N_MOD * d), F32),
        name="adaln_mod",
    )(c, w_mod, b_mod.reshape(DEPTH, 1, N_MOD * d))


def _conv_kernel(x_ref, mod_ref, g_ref, w_in_ref, cw_ref, w_out_ref, out_ref, tail_ref):
    d = D_MODEL
    ts = MIX_TOKENS

    @pl.when(pl.program_id(1) == 0)
    def _():
        tail_ref[...] = jnp.zeros_like(tail_ref)

    x = x_ref[...]
    mod = mod_ref[...]
    h = _norm_mod(x, g_ref[...], mod[:, d:2 * d], mod[:, 0:d]).astype(BF16)
    proj = jnp.dot(h, w_in_ref[...], preferred_element_type=F32)
    b_gate = proj[:, 0:d]
    v = proj[:, d:2 * d] * proj[:, 2 * d:3 * d]
    tail = tail_ref[...]
    p1 = tail[SUBLANES - 1:SUBLANES]
    p2 = tail[SUBLANES - 2:SUBLANES - 1]
    rows = lax.broadcasted_iota(jnp.int32, (ts, d), 0)
    vm1 = jnp.where(rows >= 1, pltpu.roll(v, 1, 0), p1)
    vm2 = jnp.where(rows >= 2, pltpu.roll(v, 2, 0), jnp.where(rows == 1, p1, p2))
    cw = cw_ref[...]
    conv = cw[0:1] * vm2 + cw[1:2] * vm1 + cw[2:3] * v
    tail_ref[...] = v[ts - SUBLANES:ts]
    y = jnp.dot((b_gate * conv).astype(BF16), w_out_ref[...], preferred_element_type=F32)
    out_ref[...] = x + mod[:, 2 * d:3 * d] * y


def _conv_call(x, mod_l, g, w_in, cw, w_out):
    d = D_MODEL
    ts = MIX_TOKENS
    return pl.pallas_call(
        _conv_kernel,
        grid=(BATCH, SEQ // ts),
        in_specs=[
            pl.BlockSpec((None, ts, d), lambda b, s: (b, s, 0)),
            pl.BlockSpec((None, 1, N_MOD * d), lambda b, s: (b, 0, 0)),
            pl.BlockSpec((1, d), lambda b, s: (0, 0)),
            pl.BlockSpec((d, 3 * d), lambda b, s: (0, 0)),
            pl.BlockSpec((CONV_WIDTH, d), lambda b, s: (0, 0)),
            pl.BlockSpec((d, d), lambda b, s: (0, 0)),
        ],
        out_specs=pl.BlockSpec((None, ts, d), lambda b, s: (b, s, 0)),
        out_shape=jax.ShapeDtypeStruct((BATCH, SEQ, d), F32),
        scratch_shapes=[pltpu.VMEM((SUBLANES, d), F32)],
        compiler_params=pltpu.CompilerParams(
            dimension_semantics=("arbitrary", "arbitrary"),
            vmem_limit_bytes=_vmem_limit(48 * 1024 * 1024)),
        name="conv_mixer",
    )(x, mod_l, g, w_in, cw, w_out)


def _pool_kernel(x_ref, mod_ref, g_ref, w_in_ref, w_grp_ref, scale_ref, w_out_ref, out_ref, halo_ref):
    d = D_MODEL
    ts = MIX_TOKENS
    gw = POOL_GROUP_W
    s_blk = pl.program_id(1)

    @pl.when(s_blk == 0)
    def _():
        halo_ref[...] = jnp.zeros_like(halo_ref)

    x = x_ref[...]
    mod = mod_ref[...]
    h = _norm_mod(x, g_ref[...], mod[:, d:2 * d], mod[:, 0:d]).astype(BF16)
    u = jnp.dot(h, w_in_ref[...], preferred_element_type=F32)
    ext = jnp.concatenate([halo_ref[...], u], axis=0)
    halo_ref[...] = u[ts - POOL_HALO:ts]
    t_pos = s_blk * ts + lax.broadcasted_iota(jnp.int32, (ts, gw), 0)
    zs = []
    for gi, win in enumerate(POOL_WINDOWS):
        acc = ext[:, gi * gw:(gi + 1) * gw]
        span = 1
        while span < win:
            acc = acc + pltpu.roll(acc, span, 0)
            span *= 2
        wsum = acc[POOL_HALO:POOL_HALO + ts]
        cnt = jnp.minimum(t_pos + 1, win).astype(F32)
        pooled = wsum / cnt - u[:, gi * gw:(gi + 1) * gw]
        zs.append(jnp.dot(pooled.astype(BF16), w_grp_ref[gi], preferred_element_type=F32))
    z = jnp.concatenate(zs, axis=1) * scale_ref[...]
    y = jnp.dot(z.astype(BF16), w_out_ref[...], preferred_element_type=F32)
    out_ref[...] = x + mod[:, 2 * d:3 * d] * y


def _pool_call(x, mod_l, g, w_in, w_grp, scale, w_out):
    d = D_MODEL
    ts = MIX_TOKENS
    gw = POOL_GROUP_W
    return pl.pallas_call(
        _pool_kernel,
        grid=(BATCH, SEQ // ts),
        in_specs=[
            pl.BlockSpec((None, ts, d), lambda b, s: (b, s, 0)),
            pl.BlockSpec((None, 1, N_MOD * d), lambda b, s: (b, 0, 0)),
            pl.BlockSpec((1, d), lambda b, s: (0, 0)),
            pl.BlockSpec((d, d), lambda b, s: (0, 0)),
            pl.BlockSpec((N_POOL_GROUPS, gw, gw), lambda b, s: (0, 0, 0)),
            pl.BlockSpec((1, d), lambda b, s: (0, 0)),
            pl.BlockSpec((d, d), lambda b, s: (0, 0)),
        ],
        out_specs=pl.BlockSpec((None, ts, d), lambda b, s: (b, s, 0)),
        out_shape=jax.ShapeDtypeStruct((BATCH, SEQ, d), F32),
        scratch_shapes=[pltpu.VMEM((POOL_HALO, d), F32)],
        compiler_params=pltpu.CompilerParams(
            dimension_semantics=("arbitrary", "arbitrary"),
            vmem_limit_bytes=_vmem_limit(48 * 1024 * 1024)),
        name="pool_mixer",
    )(x, mod_l, g, w_in, w_grp, scale, w_out)


def _sorting_network(n):
    pairs = []
    p = 1
    while p < n:
        k = p
        while k >= 1:
            for j in range(k % p, n - k, 2 * k):
                for i in range(min(k, n - j - k)):
                    if (i + j) // (2 * p) == (i + j + k) // (2 * p):
                        pairs.append((i + j, i + j + k))
            k //= 2
        p *= 2
    return pairs


def _top_values(s, k):
    n, t = s.shape
    groups = [s[g * SUBLANES:(g + 1) * SUBLANES] for g in range(n // SUBLANES)]
    n_net = 1
    while n_net < len(groups):
        n_net *= 2
    groups += [jnp.full((SUBLANES, t), -jnp.inf, F32)] * (n_net - len(groups))
    for i, j in _sorting_network(n_net):
        groups[i], groups[j] = jnp.maximum(groups[i], groups[j]), jnp.minimum(groups[i], groups[j])
    depth = min(len(groups), k)
    lists = groups[:depth]
    rows = []
    for r in range(k):
        m = jnp.max(lists[0], axis=0, keepdims=True)
        rows.append(m)
        live = min(depth, k - r - 1)
        hit = lists[0] >= m
        for j in range(live):
            below = lists[j + 1] if j + 1 < depth else -jnp.inf
            lists[j] = jnp.where(hit, below, lists[j])
    return rows


def _threshold_lookup(s, thresholds, values, default):
    out = jnp.full(s.shape, default, F32)
    for r in reversed(range(len(thresholds))):
        out = jnp.where(s >= thresholds[r], values[r], out)
    return out


def _pair_word(x):
    bits = lax.bitcast_convert_type(x.astype(BF16).astype(F32), U32)
    return bits | (bits >> 16)


def _router_kernel(x_ref, mod_ref, g_ref, wq_ref, k1_ref, k2_ref,
                   ht_ref, nw_ref, eaw_ref, rb_ref, eb_ref, q_ref):
    d = D_MODEL
    k = PEER_TOPK
    mod = mod_ref[...]
    h = _norm_mod(x_ref[...], g_ref[...], mod[:, 4 * d:5 * d], mod[:, 3 * d:4 * d])
    ht = h.T.astype(BF16)
    ht_ref[...] = pltpu.bitcast(ht, U32)
    q_ref[...] = jnp.dot(pltpu.bitcast(wq_ref[...], BF16), ht,
                         preferred_element_type=F32).astype(BF16)

    def head(hd, carry):
        row = pl.multiple_of(hd * D_QUERY, D_QUERY)
        s1 = jnp.dot(k1_ref[...].astype(BF16), q_ref[pl.ds(row, D_HALF), :],
                     preferred_element_type=F32)
        s2 = jnp.dot(k2_ref[...].astype(BF16), q_ref[pl.ds(row + D_HALF, D_HALF), :],
                     preferred_element_type=F32)
        v1 = _top_values(s1, k)
        v2 = _top_values(s2, k)
        v2_all = jnp.concatenate(v2, axis=0)
        pieces = [v1[r] + v2_all[0:k // (r + 1)] for r in range(k)]
        pad = -sum(p.shape[0] for p in pieces) % SUBLANES
        pieces.append(jnp.full((pad, v2_all.shape[1]), -jnp.inf, F32))
        top = _top_values(jnp.concatenate(pieces, axis=0), k)
        thr = top[k - 1]
        z = sum(jnp.exp(tr_ - top[0]) for tr_ in top)
        v1_all = jnp.concatenate(v1, axis=0)
        n_sorted = jnp.zeros_like(v1_all)
        for r2 in range(k):
            n_sorted = jnp.where(v1_all + v2[r2] >= thr, float(r2 + 1), n_sorted)
        n1 = _threshold_lookup(s1, v1, [n_sorted[r:r + 1] for r in range(k)], 0.0)
        rank2 = _threshold_lookup(s2, v2, [float(r) for r in range(k)], float(k))
        nw_ref[hd] = _pair_word(n1)
        eaw_ref[hd] = _pair_word(jnp.exp(s1 - v1[0]) / z)
        rb_ref[hd] = pltpu.bitcast(rank2.astype(BF16), U32)
        eb_ref[hd] = pltpu.bitcast(jnp.exp(s2 - v2[0]).astype(BF16), U32)
        return carry

    lax.fori_loop(0, PEER_HEADS, head, 0)


def _router_call(x_flat, mod_l, g, wq_t, k1, k2):
    d = D_MODEL
    tr = ROUTER_TOKENS
    nt = N_TOKENS
    key_spec = pl.BlockSpec((PEER_HEADS, N_KEYS, tr), lambda i: (0, 0, i))
    pair_spec = pl.BlockSpec((PEER_HEADS, N_KEYS // 2, tr), lambda i: (0, 0, i))
    return pl.pallas_call(
        _router_kernel,
        grid=(nt // tr,),
        in_specs=[
            pl.BlockSpec((tr, d), lambda i: (i, 0)),
            pl.BlockSpec((None, 1, N_MOD * d), lambda i: (i * tr // SEQ, 0, 0)),
            pl.BlockSpec((1, d), lambda i: (0, 0)),
            pl.BlockSpec((PEER_HEADS * D_QUERY // 2, d), lambda i: (0, 0)),
            pl.BlockSpec((N_KEYS, D_HALF), lambda i: (0, 0)),
            pl.BlockSpec((N_KEYS, D_HALF), lambda i: (0, 0)),
        ],
        out_specs=[
            pl.BlockSpec((d // 2, tr), lambda i: (0, i)),
            key_spec, key_spec, pair_spec, pair_spec,
        ],
        out_shape=[
            jax.ShapeDtypeStruct((d // 2, nt), U32),
            jax.ShapeDtypeStruct((PEER_HEADS, N_KEYS, nt), U32),
            jax.ShapeDtypeStruct((PEER_HEADS, N_KEYS, nt), U32),
            jax.ShapeDtypeStruct((PEER_HEADS, N_KEYS // 2, nt), U32),
            jax.ShapeDtypeStruct((PEER_HEADS, N_KEYS // 2, nt), U32),
        ],
        scratch_shapes=[pltpu.VMEM((PEER_HEADS * D_QUERY, tr), BF16)],
        compiler_params=pltpu.CompilerParams(
            dimension_semantics=("arbitrary",),
            vmem_limit_bytes=_vmem_limit(48 * 1024 * 1024)),
        name="peer_router",
    )(x_flat, mod_l, g, wq_t, k1, k2)


def _gate_pair(e, act_ref, w_ref, nw_ref, eaw_ref, rb_ref, eb_ref):
    n_sub = N_KEYS // BF16_ROWS
    for c in range(EXPERT_TOKENS // LANES):
        tc = slice(c * LANES, (c + 1) * LANES)
        gates = [[jnp.zeros((BF16_ROWS, LANES), BF16) for _ in range(n_sub)] for _ in range(2)]
        for hd in range(PEER_HEADS):
            n_rows, ea_rows = [], []
            for de in range(2):
                nw = jnp.broadcast_to(nw_ref[hd, e + de:e + de + 1, tc], (SUBLANES, LANES))
                ew = jnp.broadcast_to(eaw_ref[hd, e + de:e + de + 1, tc], (SUBLANES, LANES))
                n_rows.append(pltpu.bitcast(nw, BF16))
                ea_rows.append(pltpu.bitcast(ew, BF16))
            for kk in range(n_sub):
                ws = slice(kk * SUBLANES, (kk + 1) * SUBLANES)
                rank2 = pltpu.bitcast(rb_ref[hd, ws, tc], BF16)
                eb = pltpu.bitcast(eb_ref[hd, ws, tc], BF16)
                for de in range(2):
                    g = gates[de][kk]
                    gates[de][kk] = jnp.where(rank2 < n_rows[de], g + ea_rows[de] * eb, g)
        for de in range(2):
            for kk in range(n_sub):
                rows = slice((e + de) * N_KEYS + kk * BF16_ROWS, (e + de) * N_KEYS + (kk + 1) * BF16_ROWS)
                w_ref[rows, tc] = _gelu(act_ref[rows, tc]).astype(BF16) * gates[de][kk]


def _expert_kernel(ht_ref, u_ref, vt_ref, nw_ref, eaw_ref, rb_ref, eb_ref, x_ref, mod_ref, fg_ref,
                   out_ref, acc_ref, act0_ref, act1_ref, w0_ref, w1_ref, *, final_norm, n_steps, n_chunks):
    d = D_MODEL
    s = pl.program_id(0)
    sc = jnp.clip(s - 2, 0, n_steps - 1)
    jc = sc % n_chunks

    @pl.when(s == 0)
    def _():
        act1_ref[...] = jnp.zeros_like(act1_ref)
        w0_ref[...] = jnp.zeros_like(w0_ref)

    @pl.when(jc == 0)
    def _():
        acc_ref[...] = jnp.zeros_like(acc_ref)

    def step(act_a, act_b, w_b, w_c):
        n_pairs = E1_PER_CHUNK // 2
        pieces_a = EXPERT_CHUNK // MXU_ROWS
        pieces_c = d // MXU_ROWS
        half = MXU_ROWS // 2
        for p in range(pieces_a):
            u = pltpu.bitcast(u_ref[p * half:(p + 1) * half, :], BF16)
            act_a[p * MXU_ROWS:(p + 1) * MXU_ROWS, :] = jnp.dot(
                u, pltpu.bitcast(ht_ref[...], BF16), preferred_element_type=F32)
            for q in range(p * n_pairs // pieces_a, (p + 1) * n_pairs // pieces_a):
                _gate_pair(2 * q, act_b, w_b, nw_ref, eaw_ref, rb_ref, eb_ref)
            if (p + 1) * pieces_c % pieces_a == 0:
                pc = (p + 1) * pieces_c // pieces_a - 1
                vt = pltpu.bitcast(vt_ref[pc * half:(pc + 1) * half, :], BF16)
                acc_ref[pc * MXU_ROWS:(pc + 1) * MXU_ROWS, :] += jnp.dot(
                    vt, w_c[...], preferred_element_type=F32)

    @pl.when(s % 2 == 0)
    def _():
        step(act0_ref, act1_ref, w1_ref, w0_ref)

    @pl.when(s % 2 == 1)
    def _():
        step(act1_ref, act0_ref, w0_ref, w1_ref)

    @pl.when((jc == n_chunks - 1) & (s >= 2))
    def _():
        mod = mod_ref[...]
        xn = x_ref[...] + mod[:, 5 * d:6 * d] * acc_ref[...].T
        if final_norm:
            ms = jnp.mean(xn * xn, axis=-1, keepdims=True)
            xn = xn * lax.rsqrt(ms + EPS) * fg_ref[...]
        out_ref[...] = xn


def _expert_call(ht, u_tab, vt_tab, nw, eaw, rb, eb, x_flat, mod_l, final_g, final_norm):
    d = D_MODEL
    tb = EXPERT_TOKENS
    ec = EXPERT_CHUNK
    nt = N_TOKENS
    n_chunks = N_EXPERTS // ec
    n_steps = (nt // tb) * n_chunks

    def stage(lag):
        def split(s):
            t = jnp.clip(s - lag, 0, n_steps - 1)
            return t // n_chunks, t % n_chunks
        return split

    sa, sb, sc = stage(0), stage(1), stage(2)
    row_spec = pl.BlockSpec((PEER_HEADS, E1_PER_CHUNK, tb), lambda s: (0, sb(s)[1], sb(s)[0]))
    col_spec = pl.BlockSpec((PEER_HEADS, N_KEYS // 2, tb), lambda s: (0, 0, sb(s)[0]))
    return pl.pallas_call(
        functools.partial(_expert_kernel, final_norm=final_norm, n_steps=n_steps, n_chunks=n_chunks),
        grid=(n_steps + 2,),
        in_specs=[
            pl.BlockSpec((d // 2, tb), lambda s: (0, sa(s)[0])),
            pl.BlockSpec((ec // 2, d), lambda s: (sa(s)[1], 0)),
            pl.BlockSpec((d // 2, ec), lambda s: (0, sc(s)[1])),
            row_spec, row_spec, col_spec, col_spec,
            pl.BlockSpec((tb, d), lambda s: (sc(s)[0], 0)),
            pl.BlockSpec((None, 1, N_MOD * d), lambda s: (sc(s)[0] * tb // SEQ, 0, 0)),
            pl.BlockSpec((1, d), lambda s: (0, 0)),
        ],
        out_specs=pl.BlockSpec((tb, d), lambda s: (sc(s)[0], 0)),
        out_shape=jax.ShapeDtypeStruct((nt, d), F32),
        scratch_shapes=[
            pltpu.VMEM((d, tb), F32),
            pltpu.VMEM((ec, tb), F32),
            pltpu.VMEM((ec, tb), F32),
            pltpu.VMEM((ec, tb), BF16),
            pltpu.VMEM((ec, tb), BF16),
        ],
        compiler_params=pltpu.CompilerParams(
            dimension_semantics=("arbitrary",),
            vmem_limit_bytes=_vmem_limit(52 * 1024 * 1024)),
        name="peer_experts",
    )(ht, u_tab, vt_tab, nw, eaw, rb, eb, x_flat, mod_l, final_g)


def kernel(x, c, norm1_g, norm2_g, w_mod, b_mod, conv_w_in, conv_w, conv_w_out, pool_w_in, pool_w_grp,
           pool_scale, pool_w_out, peer_w_q, peer_k1, peer_k2, peer_u, peer_v, final_g):
    d = D_MODEL
    mod = _mod_call(c, w_mod, b_mod).reshape(DEPTH, BATCH, 1, N_MOD * d)
    fg = final_g.reshape(1, d)
    for i in range(DEPTH):
        g1 = norm1_g[i].reshape(1, d)
        g2 = norm2_g[i].reshape(1, d)
        jm = i // 2
        if i % 2 == 0:
            x = _conv_call(x, mod[i], g1, conv_w_in[jm].astype(BF16), conv_w[jm],
                           conv_w_out[jm].astype(BF16))
        else:
            x = _pool_call(x, mod[i], g1, pool_w_in[jm].astype(BF16), pool_w_grp[jm].astype(BF16),
                           pool_scale[jm].reshape(1, d), pool_w_out[jm].astype(BF16))
        x_flat = x.reshape(N_TOKENS, d)
        ht, nw, eaw, rb, eb = _router_call(x_flat, mod[i], g2, _pack_cols_call(peer_w_q, i),
                                           peer_k1[i], peer_k2[i])
        x_flat = _expert_call(ht, _pack_rows_call(peer_u, i), _pack_cols_call(peer_v, i), nw, eaw, rb, eb,
                              x_flat, mod[i], fg, final_norm=(i == DEPTH - 1))
        x = x_flat.reshape(BATCH, SEQ, d)
    return x
```

```python
import functools
import math

import jax
import jax.numpy as jnp
from jax import lax
from jax.experimental import pallas as pl
from jax.experimental.pallas import tpu as pltpu

D_MODEL = 1024
BATCH = 8
SEQ = 2048
DEPTH = 2
CONV_WIDTH = 3
POOL_WINDOWS = (2, 4, 8, 16)
N_POOL_GROUPS = len(POOL_WINDOWS)
POOL_GROUP_W = D_MODEL // N_POOL_GROUPS
N_KEYS = 128
N_EXPERTS = N_KEYS * N_KEYS
PEER_HEADS = 8
PEER_TOPK = 16
D_QUERY = 256
D_HALF = D_QUERY // 2
N_MOD = 6
EPS = 1e-6
N_TOKENS = BATCH * SEQ

V7X_VMEM_BYTES = 64 * 1024 * 1024
LANES = 128
SUBLANES = 8
BF16_ROWS = 2 * SUBLANES

MIX_TOKENS = 512
ROUTER_TOKENS = 256
EXPERT_TOKENS = 512
EXPERT_CHUNK = 2048
E1_PER_CHUNK = EXPERT_CHUNK // N_KEYS
MXU_ROWS = 512
GATE_ROWS = 4
GATE_SUBS = 4
POOL_HALO = 16

F32 = jnp.float32
BF16 = jnp.bfloat16
U32 = jnp.uint32


def _vmem_limit(nbytes):
    return int(min(nbytes, V7X_VMEM_BYTES - 8 * 1024 * 1024))


def _norm_mod(x, g, scale, shift):
    ms = jnp.mean(x * x, axis=-1, keepdims=True)
    y = x * lax.rsqrt(ms + EPS)
    return (y * g) * (1.0 + scale) + shift


def _gelu(x):
    return 0.5 * x * (1.0 + lax.erf(x * (1.0 / math.sqrt(2.0))))


PACK_ROWS = 1024


def _pack_rows_kernel(x_ref, o_ref):
    o_ref[...] = pltpu.bitcast(x_ref[...].astype(BF16), U32)


def _pack_rows_call(x, layer):
    _, n, m = x.shape
    return pl.pallas_call(
        _pack_rows_kernel,
        grid=(n // PACK_ROWS,),
        in_specs=[pl.BlockSpec((None, PACK_ROWS, m), lambda i: (layer, i, 0))],
        out_specs=pl.BlockSpec((PACK_ROWS // 2, m), lambda i: (i, 0)),
        out_shape=jax.ShapeDtypeStruct((n // 2, m), U32),
        name="pack_rows",
    )(x)


def _pack_cols_kernel(x_ref, o_ref):
    o_ref[...] = pltpu.bitcast(x_ref[...].T.astype(BF16), U32)


def _pack_cols_call(x, layer):
    _, n, m = x.shape
    return pl.pallas_call(
        _pack_cols_kernel,
        grid=(n // PACK_ROWS,),
        in_specs=[pl.BlockSpec((None, PACK_ROWS, m), lambda i: (layer, i, 0))],
        out_specs=pl.BlockSpec((m // 2, PACK_ROWS), lambda i: (0, i)),
        out_shape=jax.ShapeDtypeStruct((m // 2, n), U32),
        name="pack_cols",
    )(x)


def _mod_kernel(c_ref, w_ref, b_ref, out_ref):
    c = c_ref[...]
    c_act = c * jax.nn.sigmoid(c)
    out_ref[...] = jnp.dot(c_act, w_ref[...], preferred_element_type=F32,
                           precision=lax.Precision.HIGHEST) + b_ref[...]


def _mod_call(c, w_mod, b_mod):
    d = D_MODEL
    return pl.pallas_call(
        _mod_kernel,
        grid=(DEPTH, N_MOD),
        in_specs=[
            pl.BlockSpec((BATCH, d), lambda l, n: (0, 0)),
            pl.BlockSpec((None, d, d), lambda l, n: (l, 0, n)),
            pl.BlockSpec((None, 1, d), lambda l, n: (l, 0, n)),
        ],
        out_specs=pl.BlockSpec((None, BATCH, d), lambda l, n: (l, 0, n)),
        out_shape=jax.ShapeDtypeStruct((DEPTH, BATCH, N_MOD * d), F32),
        name="adaln_mod",
    )(c, w_mod, b_mod.reshape(DEPTH, 1, N_MOD * d))


def _conv_kernel(x_ref, mod_ref, g_ref, w_in_ref, cw_ref, w_out_ref, out_ref, tail_ref):
    d = D_MODEL
    ts = MIX_TOKENS

    @pl.when(pl.program_id(1) == 0)
    def _():
        tail_ref[...] = jnp.zeros_like(tail_ref)

    x = x_ref[...]
    mod = mod_ref[...]
    h = _norm_mod(x, g_ref[...], mod[:, d:2 * d], mod[:, 0:d]).astype(BF16)
    proj = jnp.dot(h, w_in_ref[...], preferred_element_type=F32)
    b_gate = proj[:, 0:d]
    v = proj[:, d:2 * d] * proj[:, 2 * d:3 * d]
    tail = tail_ref[...]
    p1 = tail[SUBLANES - 1:SUBLANES]
    p2 = tail[SUBLANES - 2:SUBLANES - 1]
    rows = lax.broadcasted_iota(jnp.int32, (ts, d), 0)
    vm1 = jnp.where(rows >= 1, pltpu.roll(v, 1, 0), p1)
    vm2 = jnp.where(rows >= 2, pltpu.roll(v, 2, 0), jnp.where(rows == 1, p1, p2))
    cw = cw_ref[...]
    conv = cw[0:1] * vm2 + cw[1:2] * vm1 + cw[2:3] * v
    tail_ref[...] = v[ts - SUBLANES:ts]
    y = jnp.dot((b_gate * conv).astype(BF16), w_out_ref[...], preferred_element_type=F32)
    out_ref[...] = x + mod[:, 2 * d:3 * d] * y


def _conv_call(x, mod_l, g, w_in, cw, w_out):
    d = D_MODEL
    ts = MIX_TOKENS
    return pl.pallas_call(
        _conv_kernel,
        grid=(BATCH, SEQ // ts),
        in_specs=[
            pl.BlockSpec((None, ts, d), lambda b, s: (b, s, 0)),
            pl.BlockSpec((None, 1, N_MOD * d), lambda b, s: (b, 0, 0)),
            pl.BlockSpec((1, d), lambda b, s: (0, 0)),
            pl.BlockSpec((d, 3 * d), lambda b, s: (0, 0)),
            pl.BlockSpec((CONV_WIDTH, d), lambda b, s: (0, 0)),
            pl.BlockSpec((d, d), lambda b, s: (0, 0)),
        ],
        out_specs=pl.BlockSpec((None, ts, d), lambda b, s: (b, s, 0)),
        out_shape=jax.ShapeDtypeStruct((BATCH, SEQ, d), F32),
        scratch_shapes=[pltpu.VMEM((SUBLANES, d), F32)],
        compiler_params=pltpu.CompilerParams(
            dimension_semantics=("arbitrary", "arbitrary"),
            vmem_limit_bytes=_vmem_limit(48 * 1024 * 1024)),
        name="conv_mixer",
    )(x, mod_l, g, w_in, cw, w_out)


def _pool_kernel(x_ref, mod_ref, g_ref, w_in_ref, w_grp_ref, scale_ref, w_out_ref, out_ref, halo_ref):
    d = D_MODEL
    ts = MIX_TOKENS
    gw = POOL_GROUP_W
    s_blk = pl.program_id(1)

    @pl.when(s_blk == 0)
    def _():
        halo_ref[...] = jnp.zeros_like(halo_ref)

    x = x_ref[...]
    mod = mod_ref[...]
    h = _norm_mod(x, g_ref[...], mod[:, d:2 * d], mod[:, 0:d]).astype(BF16)
    u = jnp.dot(h, w_in_ref[...], preferred_element_type=F32)
    ext = jnp.concatenate([halo_ref[...], u], axis=0)
    halo_ref[...] = u[ts - POOL_HALO:ts]
    t_pos = s_blk * ts + lax.broadcasted_iota(jnp.int32, (ts, gw), 0)
    zs = []
    for gi, win in enumerate(POOL_WINDOWS):
        acc = ext[:, gi * gw:(gi + 1) * gw]
        span = 1
        while span < win:
            acc = acc + pltpu.roll(acc, span, 0)
            span *= 2
        wsum = acc[POOL_HALO:POOL_HALO + ts]
        cnt = jnp.minimum(t_pos + 1, win).astype(F32)
        pooled = wsum / cnt - u[:, gi * gw:(gi + 1) * gw]
        zs.append(jnp.dot(pooled.astype(BF16), w_grp_ref[gi], preferred_element_type=F32))
    z = jnp.concatenate(zs, axis=1) * scale_ref[...]
    y = jnp.dot(z.astype(BF16), w_out_ref[...], preferred_element_type=F32)
    out_ref[...] = x + mod[:, 2 * d:3 * d] * y


def _pool_call(x, mod_l, g, w_in, w_grp, scale, w_out):
    d = D_MODEL
    ts = MIX_TOKENS
    gw = POOL_GROUP_W
    return pl.pallas_call(
        _pool_kernel,
        grid=(BATCH, SEQ // ts),
        in_specs=[
            pl.BlockSpec((None, ts, d), lambda b, s: (b, s, 0)),
            pl.BlockSpec((None, 1, N_MOD * d), lambda b, s: (b, 0, 0)),
            pl.BlockSpec((1, d), lambda b, s: (0, 0)),
            pl.BlockSpec((d, d), lambda b, s: (0, 0)),
            pl.BlockSpec((N_POOL_GROUPS, gw, gw), lambda b, s: (0, 0, 0)),
            pl.BlockSpec((1, d), lambda b, s: (0, 0)),
            pl.BlockSpec((d, d), lambda b, s: (0, 0)),
        ],
        out_specs=pl.BlockSpec((None, ts, d), lambda b, s: (b, s, 0)),
        out_shape=jax.ShapeDtypeStruct((BATCH, SEQ, d), F32),
        scratch_shapes=[pltpu.VMEM((POOL_HALO, d), F32)],
        compiler_params=pltpu.CompilerParams(
            dimension_semantics=("arbitrary", "arbitrary"),
            vmem_limit_bytes=_vmem_limit(48 * 1024 * 1024)),
        name="pool_mixer",
    )(x, mod_l, g, w_in, w_grp, scale, w_out)


def _sorting_network(n):
    pairs = []
    p = 1
    while p < n:
        k = p
        while k >= 1:
            for j in range(k % p, n - k, 2 * k):
                for i in range(min(k, n - j - k)):
                    if (i + j) // (2 * p) == (i + j + k) // (2 * p):
                        pairs.append((i + j, i + j + k))
            k //= 2
        p *= 2
    return pairs


def _top_values(s, k):
    n, t = s.shape
    groups = [s[g * SUBLANES:(g + 1) * SUBLANES] for g in range(n // SUBLANES)]
    n_net = 1
    while n_net < len(groups):
        n_net *= 2
    groups += [jnp.full((SUBLANES, t), -jnp.inf, F32)] * (n_net - len(groups))
    for i, j in _sorting_network(n_net):
        groups[i], groups[j] = jnp.maximum(groups[i], groups[j]), jnp.minimum(groups[i], groups[j])
    depth = min(len(groups), k)
    lists = groups[:depth]
    rows = []
    for r in range(k):
        m = jnp.max(lists[0], axis=0, keepdims=True)
        rows.append(m)
        live = min(depth, k - r - 1)
        hit = lists[0] >= m
        for j in range(live):
            below = lists[j + 1] if j + 1 < depth else -jnp.inf
            lists[j] = jnp.where(hit, below, lists[j])
    return rows


def _threshold_lookup(s, thresholds, values, default):
    out = jnp.full(s.shape, default, F32)
    for r in reversed(range(len(thresholds))):
        out = jnp.where(s >= thresholds[r], values[r], out)
    return out


def _pair_word(x):
    bits = lax.bitcast_convert_type(x.astype(BF16).astype(F32), U32)
    return bits | (bits >> 16)


def _router_kernel(x_ref, mod_ref, g_ref, wq_ref, k1_ref, k2_ref,
                   ht_ref, nw_ref, eaw_ref, rb_ref, eb_ref, q_ref):
    d = D_MODEL
    k = PEER_TOPK
    mod = mod_ref[...]
    h = _norm_mod(x_ref[...], g_ref[...], mod[:, 4 * d:5 * d], mod[:, 3 * d:4 * d])
    ht = h.T.astype(BF16)
    ht_ref[...] = pltpu.bitcast(ht, U32)
    q_ref[...] = jnp.dot(pltpu.bitcast(wq_ref[...], BF16), ht,
                         preferred_element_type=F32).astype(BF16)

    def head(hd, carry):
        row = pl.multiple_of(hd * D_QUERY, D_QUERY)
        s1 = jnp.dot(k1_ref[...].astype(BF16), q_ref[pl.ds(row, D_HALF), :],
                     preferred_element_type=F32)
        s2 = jnp.dot(k2_ref[...].astype(BF16), q_ref[pl.ds(row + D_HALF, D_HALF), :],
                     preferred_element_type=F32)
        v1 = _top_values(s1, k)
        v2 = _top_values(s2, k)
        v2_all = jnp.concatenate(v2, axis=0)
        pieces = [v1[r] + v2_all[0:k // (r + 1)] for r in range(k)]
        pad = -sum(p.shape[0] for p in pieces) % SUBLANES
        pieces.append(jnp.full((pad, v2_all.shape[1]), -jnp.inf, F32))
        top = _top_values(jnp.concatenate(pieces, axis=0), k)
        thr = top[k - 1]
        z = sum(jnp.exp(tr_ - top[0]) for tr_ in top)
        v1_all = jnp.concatenate(v1, axis=0)
        n_sorted = jnp.zeros_like(v1_all)
        for r2 in range(k):
            n_sorted = jnp.where(v1_all + v2[r2] >= thr, float(r2 + 1), n_sorted)
        n1 = _threshold_lookup(s1, v1, [n_sorted[r:r + 1] for r in range(k)], 0.0)
        rank2 = _threshold_lookup(s2, v2, [float(r) for r in range(k)], float(k))
        nw_ref[hd] = _pair_word(n1)
        eaw_ref[hd] = _pair_word(jnp.exp(s1 - v1[0]) / z)
        rb_ref[hd] = pltpu.bitcast(rank2.astype(BF16), U32)
        eb_ref[hd] = pltpu.bitcast(jnp.exp(s2 - v2[0]).astype(BF16), U32)
        return carry

    lax.fori_loop(0, PEER_HEADS, head, 0)


def _router_call(x_flat, mod_l, g, wq_t, k1, k2):
    d = D_MODEL
    tr = ROUTER_TOKENS
    nt = N_TOKENS
    key_spec = pl.BlockSpec((PEER_HEADS, N_KEYS, tr), lambda i: (0, 0, i))
    pair_spec = pl.BlockSpec((PEER_HEADS, N_KEYS // 2, tr), lambda i: (0, 0, i))
    return pl.pallas_call(
        _router_kernel,
        grid=(nt // tr,),
        in_specs=[
            pl.BlockSpec((tr, d), lambda i: (i, 0)),
            pl.BlockSpec((None, 1, N_MOD * d), lambda i: (i * tr // SEQ, 0, 0)),
            pl.BlockSpec((1, d), lambda i: (0, 0)),
            pl.BlockSpec((PEER_HEADS * D_QUERY // 2, d), lambda i: (0, 0)),
            pl.BlockSpec((N_KEYS, D_HALF), lambda i: (0, 0)),
            pl.BlockSpec((N_KEYS, D_HALF), lambda i: (0, 0)),
        ],
        out_specs=[
            pl.BlockSpec((d // 2, tr), lambda i: (0, i)),
            key_spec, key_spec, pair_spec, pair_spec,
        ],
        out_shape=[
            jax.ShapeDtypeStruct((d // 2, nt), U32),
            jax.ShapeDtypeStruct((PEER_HEADS, N_KEYS, nt), U32),
            jax.ShapeDtypeStruct((PEER_HEADS, N_KEYS, nt), U32),
            jax.ShapeDtypeStruct((PEER_HEADS, N_KEYS // 2, nt), U32),
            jax.ShapeDtypeStruct((PEER_HEADS, N_KEYS // 2, nt), U32),
        ],
        scratch_shapes=[pltpu.VMEM((PEER_HEADS * D_QUERY, tr), BF16)],
        compiler_params=pltpu.CompilerParams(
            dimension_semantics=("arbitrary",),
            vmem_limit_bytes=_vmem_limit(48 * 1024 * 1024)),
        name="peer_router",
    )(x_flat, mod_l, g, wq_t, k1, k2)


def _gate_pair(e, act_ref, w_ref, nw_ref, eaw_ref, rb_ref, eb_ref):
    n_sub = N_KEYS // BF16_ROWS
    for c in range(EXPERT_TOKENS // LANES):
        tc = slice(c * LANES, (c + 1) * LANES)
        for k0 in range(0, n_sub, GATE_SUBS):
            subs = range(k0, k0 + GATE_SUBS)
            gates = [{kk: jnp.zeros((BF16_ROWS, LANES), BF16) for kk in subs} for _ in range(GATE_ROWS)]
            for hd in range(PEER_HEADS):
                n_rows, ea_rows = [], []
                for de in range(GATE_ROWS):
                    nw = jnp.broadcast_to(nw_ref[hd, e + de:e + de + 1, tc], (SUBLANES, LANES))
                    ew = jnp.broadcast_to(eaw_ref[hd, e + de:e + de + 1, tc], (SUBLANES, LANES))
                    n_rows.append(pltpu.bitcast(nw, BF16))
                    ea_rows.append(pltpu.bitcast(ew, BF16))
                for kk in subs:
                    ws = slice(kk * SUBLANES, (kk + 1) * SUBLANES)
                    rank2 = pltpu.bitcast(rb_ref[hd, ws, tc], BF16)
                    eb = pltpu.bitcast(eb_ref[hd, ws, tc], BF16)
                    for de in range(GATE_ROWS):
                        g = gates[de][kk]
                        gates[de][kk] = jnp.where(rank2 < n_rows[de], g + ea_rows[de] * eb, g)
            for de in range(GATE_ROWS):
                for kk in subs:
                    rows = slice((e + de) * N_KEYS + kk * BF16_ROWS, (e + de) * N_KEYS + (kk + 1) * BF16_ROWS)
                    w_ref[rows, tc] = _gelu(act_ref[rows, tc]).astype(BF16) * gates[de][kk]


def _expert_kernel(ht_ref, u_ref, vt_ref, nw_ref, eaw_ref, rb_ref, eb_ref, x_ref, mod_ref, fg_ref,
                   out_ref, acc_ref, act0_ref, act1_ref, w0_ref, w1_ref, *, final_norm, n_steps, n_chunks):
    d = D_MODEL
    s = pl.program_id(0)
    sc = jnp.clip(s - 2, 0, n_steps - 1)
    jc = sc % n_chunks

    @pl.when(s == 0)
    def _():
        act1_ref[...] = jnp.zeros_like(act1_ref)
        w0_ref[...] = jnp.zeros_like(w0_ref)

    @pl.when(jc == 0)
    def _():
        acc_ref[...] = jnp.zeros_like(acc_ref)

    def step(act_a, act_b, w_b, w_c):
        n_pairs = E1_PER_CHUNK // GATE_ROWS
        pieces_a = EXPERT_CHUNK // MXU_ROWS
        pieces_c = d // MXU_ROWS
        half = MXU_ROWS // 2
        for p in range(pieces_a):
            u = pltpu.bitcast(u_ref[p * half:(p + 1) * half, :], BF16)
            act_a[p * MXU_ROWS:(p + 1) * MXU_ROWS, :] = jnp.dot(
                u, pltpu.bitcast(ht_ref[...], BF16), preferred_element_type=F32)
            for q in range(p * n_pairs // pieces_a, (p + 1) * n_pairs // pieces_a):
                _gate_pair(GATE_ROWS * q, act_b, w_b, nw_ref, eaw_ref, rb_ref, eb_ref)
            if (p + 1) * pieces_c % pieces_a == 0:
                pc = (p + 1) * pieces_c // pieces_a - 1
                vt = pltpu.bitcast(vt_ref[pc * half:(pc + 1) * half, :], BF16)
                acc_ref[pc * MXU_ROWS:(pc + 1) * MXU_ROWS, :] += jnp.dot(
                    vt, w_c[...], preferred_element_type=F32)

    @pl.when(s % 2 == 0)
    def _():
        step(act0_ref, act1_ref, w1_ref, w0_ref)

    @pl.when(s % 2 == 1)
    def _():
        step(act1_ref, act0_ref, w0_ref, w1_ref)

    @pl.when((jc == n_chunks - 1) & (s >= 2))
    def _():
        mod = mod_ref[...]
        xn = x_ref[...] + mod[:, 5 * d:6 * d] * acc_ref[...].T
        if final_norm:
            ms = jnp.mean(xn * xn, axis=-1, keepdims=True)
            xn = xn * lax.rsqrt(ms + EPS) * fg_ref[...]
        out_ref[...] = xn


def _expert_call(ht, u_tab, vt_tab, nw, eaw, rb, eb, x_flat, mod_l, final_g, final_norm):
    d = D_MODEL
    tb = EXPERT_TOKENS
    ec = EXPERT_CHUNK
    nt = N_TOKENS
    n_chunks = N_EXPERTS // ec
    n_steps = (nt // tb) * n_chunks

    def stage(lag):
        def split(s):
            t = jnp.clip(s - lag, 0, n_steps - 1)
            return t // n_chunks, t % n_chunks
        return split

    sa, sb, sc = stage(0), stage(1), stage(2)
    row_spec = pl.BlockSpec((PEER_HEADS, E1_PER_CHUNK, tb), lambda s: (0, sb(s)[1], sb(s)[0]))
    col_spec = pl.BlockSpec((PEER_HEADS, N_KEYS // 2, tb), lambda s: (0, 0, sb(s)[0]))
    return pl.pallas_call(
        functools.partial(_expert_kernel, final_norm=final_norm, n_steps=n_steps, n_chunks=n_chunks),
        grid=(n_steps + 2,),
        in_specs=[
            pl.BlockSpec((d // 2, tb), lambda s: (0, sa(s)[0])),
            pl.BlockSpec((ec // 2, d), lambda s: (sa(s)[1], 0)),
            pl.BlockSpec((d // 2, ec), lambda s: (0, sc(s)[1])),
            row_spec, row_spec, col_spec, col_spec,
            pl.BlockSpec((tb, d), lambda s: (sc(s)[0], 0)),
            pl.BlockSpec((None, 1, N_MOD * d), lambda s: (sc(s)[0] * tb // SEQ, 0, 0)),
            pl.BlockSpec((1, d), lambda s: (0, 0)),
        ],
        out_specs=pl.BlockSpec((tb, d), lambda s: (sc(s)[0], 0)),
        out_shape=jax.ShapeDtypeStruct((nt, d), F32),
        scratch_shapes=[
            pltpu.VMEM((d, tb), F32),
            pltpu.VMEM((ec, tb), F32),
            pltpu.VMEM((ec, tb), F32),
            pltpu.VMEM((ec, tb), BF16),
            pltpu.VMEM((ec, tb), BF16),
        ],
        compiler_params=pltpu.CompilerParams(
            dimension_semantics=("arbitrary",),
            vmem_limit_bytes=_vmem_limit(52 * 1024 * 1024)),
        name="peer_experts",
    )(ht, u_tab, vt_tab, nw, eaw, rb, eb, x_flat, mod_l, final_g)


def kernel(x, c, norm1_g, norm2_g, w_mod, b_mod, conv_w_in, conv_w, conv_w_out, pool_w_in, pool_w_grp,
           pool_scale, pool_w_out, peer_w_q, peer_k1, peer_k2, peer_u, peer_v, final_g):
    d = D_MODEL
    mod = _mod_call(c, w_mod, b_mod).reshape(DEPTH, BATCH, 1, N_MOD * d)
    fg = final_g.reshape(1, d)
    for i in range(DEPTH):
        g1 = norm1_g[i].reshape(1, d)
        g2 = norm2_g[i].reshape(1, d)
        jm = i // 2
        if i % 2 == 0:
            x = _conv_call(x, mod[i], g1, conv_w_in[jm].astype(BF16), conv_w[jm],
                           conv_w_out[jm].astype(BF16))
        else:
            x = _pool_call(x, mod[i], g1, pool_w_in[jm].astype(BF16), pool_w_grp[jm].astype(BF16),
                           pool_scale[jm].reshape(1, d), pool_w_out[jm].astype(BF16))
        x_flat = x.reshape(N_TOKENS, d)
        ht, nw, eaw, rb, eb = _router_call(x_flat, mod[i], g2, _pack_cols_call(peer_w_q, i),
                                           peer_k1[i], peer_k2[i])
        x_flat = _expert_call(ht, _pack_rows_call(peer_u, i), _pack_cols_call(peer_v, i), nw, eaw, rb, eb,
                              x_flat, mod[i], fg, final_norm=(i == DEPTH - 1))
        x = x_flat.reshape(BATCH, SEQ, d)
    return x
```

```python
import functools
import math

import jax
import jax.numpy as jnp
from jax import lax
from jax.experimental import pallas as pl
from jax.experimental.pallas import tpu as pltpu

D_MODEL = 1024
BATCH = 8
SEQ = 2048
DEPTH = 2
CONV_WIDTH = 3
POOL_WINDOWS = (2, 4, 8, 16)
N_POOL_GROUPS = len(POOL_WINDOWS)
POOL_GROUP_W = D_MODEL // N_POOL_GROUPS
N_KEYS = 128
N_EXPERTS = N_KEYS * N_KEYS
PEER_HEADS = 8
PEER_TOPK = 16
D_QUERY = 256
D_HALF = D_QUERY // 2
N_MOD = 6
EPS = 1e-6
N_TOKENS = BATCH * SEQ

V7X_VMEM_BYTES = 64 * 1024 * 1024
LANES = 128
SUBLANES = 8
BF16_ROWS = 2 * SUBLANES

MIX_TOKENS = 512
ROUTER_TOKENS = 512
EXPERT_TOKENS = 512
EXPERT_CHUNK = 2048
E1_PER_CHUNK = EXPERT_CHUNK // N_KEYS
MXU_ROWS = 1024
GATE_ROWS = 4
GATE_SUBS = 4
POOL_HALO = 16

F32 = jnp.float32
BF16 = jnp.bfloat16
U32 = jnp.uint32


def _vmem_limit(nbytes):
    return int(min(nbytes, V7X_VMEM_BYTES - 8 * 1024 * 1024))


def _norm_mod(x, g, scale, shift):
    ms = jnp.mean(x * x, axis=-1, keepdims=True)
    y = x * lax.rsqrt(ms + EPS)
    return (y * g) * (1.0 + scale) + shift


def _gelu(x):
    return 0.5 * x * (1.0 + lax.erf(x * (1.0 / math.sqrt(2.0))))


PACK_ROWS = 1024


def _pack_rows_kernel(x_ref, o_ref):
    o_ref[...] = pltpu.bitcast(x_ref[...].astype(BF16), U32)


def _pack_rows_call(x, layer):
    _, n, m = x.shape
    return pl.pallas_call(
        _pack_rows_kernel,
        grid=(n // PACK_ROWS,),
        in_specs=[pl.BlockSpec((None, PACK_ROWS, m), lambda i: (layer, i, 0))],
        out_specs=pl.BlockSpec((PACK_ROWS // 2, m), lambda i: (i, 0)),
        out_shape=jax.ShapeDtypeStruct((n // 2, m), U32),
        name="pack_rows",
    )(x)


def _pack_cols_kernel(x_ref, o_ref):
    o_ref[...] = pltpu.bitcast(x_ref[...].T.astype(BF16), U32)


def _pack_cols_call(x, layer):
    _, n, m = x.shape
    return pl.pallas_call(
        _pack_cols_kernel,
        grid=(n // PACK_ROWS,),
        in_specs=[pl.BlockSpec((None, PACK_ROWS, m), lambda i: (layer, i, 0))],
        out_specs=pl.BlockSpec((m // 2, PACK_ROWS), lambda i: (0, i)),
        out_shape=jax.ShapeDtypeStruct((m // 2, n), U32),
        name="pack_cols",
    )(x)


def _mod_kernel(c_ref, w_ref, b_ref, out_ref):
    c = c_ref[...]
    c_act = c * jax.nn.sigmoid(c)
    out_ref[...] = jnp.dot(c_act, w_ref[...], preferred_element_type=F32,
                           precision=lax.Precision.HIGHEST) + b_ref[...]


def _mod_call(c, w_mod, b_mod):
    d = D_MODEL
    return pl.pallas_call(
        _mod_kernel,
        grid=(DEPTH, N_MOD),
        in_specs=[
            pl.BlockSpec((BATCH, d), lambda l, n: (0, 0)),
            pl.BlockSpec((None, d, d), lambda l, n: (l, 0, n)),
            pl.BlockSpec((None, 1, d), lambda l, n: (l, 0, n)),
        ],
        out_specs=pl.BlockSpec((None, BATCH, d), lambda l, n: (l, 0, n)),
        out_shape=jax.ShapeDtypeStruct((DEPTH, BATCH, N_MOD * d), F32),
        name="adaln_mod",
    )(c, w_mod, b_mod.reshape(DEPTH, 1, N_MOD * d))


def _mixer_outputs(x_new, mod, g2_ref, out_ref, ht_ref):
    d = D_MODEL
    out_ref[...] = x_new
    h = _norm_mod(x_new, g2_ref[...], mod[:, 4 * d:5 * d], mod[:, 3 * d:4 * d])
    ht_ref[...] = pltpu.bitcast(h.T.astype(BF16), U32)


def _mixer_out_specs():
    d = D_MODEL
    ts = MIX_TOKENS
    specs = [pl.BlockSpec((None, ts, d), lambda b, s: (b, s, 0)),
             pl.BlockSpec((d // 2, ts), lambda b, s: (0, b * (SEQ // ts) + s))]
    shapes = [jax.ShapeDtypeStruct((BATCH, SEQ, d), F32),
              jax.ShapeDtypeStruct((d // 2, N_TOKENS), U32)]
    return specs, shapes


def _conv_kernel(x_ref, mod_ref, g_ref, g2_ref, w_in_ref, cw_ref, w_out_ref, out_ref, ht_ref, tail_ref):
    d = D_MODEL
    ts = MIX_TOKENS

    @pl.when(pl.program_id(1) == 0)
    def _():
        tail_ref[...] = jnp.zeros_like(tail_ref)

    x = x_ref[...]
    mod = mod_ref[...]
    h = _norm_mod(x, g_ref[...], mod[:, d:2 * d], mod[:, 0:d]).astype(BF16)
    proj = jnp.dot(h, w_in_ref[...], preferred_element_type=F32)
    b_gate = proj[:, 0:d]
    v = proj[:, d:2 * d] * proj[:, 2 * d:3 * d]
    tail = tail_ref[...]
    p1 = tail[SUBLANES - 1:SUBLANES]
    p2 = tail[SUBLANES - 2:SUBLANES - 1]
    rows = lax.broadcasted_iota(jnp.int32, (ts, d), 0)
    vm1 = jnp.where(rows >= 1, pltpu.roll(v, 1, 0), p1)
    vm2 = jnp.where(rows >= 2, pltpu.roll(v, 2, 0), jnp.where(rows == 1, p1, p2))
    cw = cw_ref[...]
    conv = cw[0:1] * vm2 + cw[1:2] * vm1 + cw[2:3] * v
    tail_ref[...] = v[ts - SUBLANES:ts]
    y = jnp.dot((b_gate * conv).astype(BF16), w_out_ref[...], preferred_element_type=F32)
    _mixer_outputs(x + mod[:, 2 * d:3 * d] * y, mod, g2_ref, out_ref, ht_ref)


def _conv_call(x, mod_l, g, g2, w_in, cw, w_out):
    d = D_MODEL
    ts = MIX_TOKENS
    out_specs, out_shape = _mixer_out_specs()
    return pl.pallas_call(
        _conv_kernel,
        grid=(BATCH, SEQ // ts),
        in_specs=[
            pl.BlockSpec((None, ts, d), lambda b, s: (b, s, 0)),
            pl.BlockSpec((None, 1, N_MOD * d), lambda b, s: (b, 0, 0)),
            pl.BlockSpec((1, d), lambda b, s: (0, 0)),
            pl.BlockSpec((1, d), lambda b, s: (0, 0)),
            pl.BlockSpec((d, 3 * d), lambda b, s: (0, 0)),
            pl.BlockSpec((CONV_WIDTH, d), lambda b, s: (0, 0)),
            pl.BlockSpec((d, d), lambda b, s: (0, 0)),
        ],
        out_specs=out_specs,
        out_shape=out_shape,
        scratch_shapes=[pltpu.VMEM((SUBLANES, d), F32)],
        compiler_params=pltpu.CompilerParams(
            dimension_semantics=("arbitrary", "arbitrary"),
            vmem_limit_bytes=_vmem_limit(48 * 1024 * 1024)),
        name="conv_mixer",
    )(x, mod_l, g, g2, w_in, cw, w_out)


def _pool_kernel(x_ref, mod_ref, g_ref, g2_ref, w_in_ref, w_grp_ref, scale_ref, w_out_ref,
                 out_ref, ht_ref, halo_ref):
    d = D_MODEL
    ts = MIX_TOKENS
    gw = POOL_GROUP_W
    s_blk = pl.program_id(1)

    @pl.when(s_blk == 0)
    def _():
        halo_ref[...] = jnp.zeros_like(halo_ref)

    x = x_ref[...]
    mod = mod_ref[...]
    h = _norm_mod(x, g_ref[...], mod[:, d:2 * d], mod[:, 0:d]).astype(BF16)
    u = jnp.dot(h, w_in_ref[...], preferred_element_type=F32)
    ext = jnp.concatenate([halo_ref[...], u], axis=0)
    halo_ref[...] = u[ts - POOL_HALO:ts]
    t_pos = s_blk * ts + lax.broadcasted_iota(jnp.int32, (ts, gw), 0)
    zs = []
    for gi, win in enumerate(POOL_WINDOWS):
        acc = ext[:, gi * gw:(gi + 1) * gw]
        span = 1
        while span < win:
            acc = acc + pltpu.roll(acc, span, 0)
            span *= 2
        wsum = acc[POOL_HALO:POOL_HALO + ts]
        cnt = jnp.minimum(t_pos + 1, win).astype(F32)
        pooled = wsum / cnt - u[:, gi * gw:(gi + 1) * gw]
        zs.append(jnp.dot(pooled.astype(BF16), w_grp_ref[gi], preferred_element_type=F32))
    z = jnp.concatenate(zs, axis=1) * scale_ref[...]
    y = jnp.dot(z.astype(BF16), w_out_ref[...], preferred_element_type=F32)
    _mixer_outputs(x + mod[:, 2 * d:3 * d] * y, mod, g2_ref, out_ref, ht_ref)


def _pool_call(x, mod_l, g, g2, w_in, w_grp, scale, w_out):
    d = D_MODEL
    ts = MIX_TOKENS
    gw = POOL_GROUP_W
    out_specs, out_shape = _mixer_out_specs()
    return pl.pallas_call(
        _pool_kernel,
        grid=(BATCH, SEQ // ts),
        in_specs=[
            pl.BlockSpec((None, ts, d), lambda b, s: (b, s, 0)),
            pl.BlockSpec((None, 1, N_MOD * d), lambda b, s: (b, 0, 0)),
            pl.BlockSpec((1, d), lambda b, s: (0, 0)),
            pl.BlockSpec((1, d), lambda b, s: (0, 0)),
            pl.BlockSpec((d, d), lambda b, s: (0, 0)),
            pl.BlockSpec((N_POOL_GROUPS, gw, gw), lambda b, s: (0, 0, 0)),
            pl.BlockSpec((1, d), lambda b, s: (0, 0)),
            pl.BlockSpec((d, d), lambda b, s: (0, 0)),
        ],
        out_specs=out_specs,
        out_shape=out_shape,
        scratch_shapes=[pltpu.VMEM((POOL_HALO, d), F32)],
        compiler_params=pltpu.CompilerParams(
            dimension_semantics=("arbitrary", "arbitrary"),
            vmem_limit_bytes=_vmem_limit(48 * 1024 * 1024)),
        name="pool_mixer",
    )(x, mod_l, g, g2, w_in, w_grp, scale, w_out)


def _sorting_network(n):
    pairs = []
    p = 1
    while p < n:
        k = p
        while k >= 1:
            for j in range(k % p, n - k, 2 * k):
                for i in range(min(k, n - j - k)):
                    if (i + j) // (2 * p) == (i + j + k) // (2 * p):
                        pairs.append((i + j, i + j + k))
            k //= 2
        p *= 2
    return pairs


def _top_values(s, k):
    n, t = s.shape
    groups = [s[g * SUBLANES:(g + 1) * SUBLANES] for g in range(n // SUBLANES)]
    n_net = 1
    while n_net < len(groups):
        n_net *= 2
    groups += [jnp.full((SUBLANES, t), -jnp.inf, F32)] * (n_net - len(groups))
    for i, j in _sorting_network(n_net):
        groups[i], groups[j] = jnp.maximum(groups[i], groups[j]), jnp.minimum(groups[i], groups[j])
    depth = min(len(groups), k)
    lists = groups[:depth]
    rows = []
    for r in range(k):
        m = jnp.max(lists[0], axis=0, keepdims=True)
        rows.append(m)
        live = min(depth, k - r - 1)
        hit = lists[0] >= m
        for j in range(live):
            below = lists[j + 1] if j + 1 < depth else -jnp.inf
            lists[j] = jnp.where(hit, below, lists[j])
    return rows


def _row(ref, i):
    return jnp.broadcast_to(ref[i:i + 1, :], (SUBLANES, ref.shape[-1]))


def _threshold_lookup(s, thr_ref, value, default):
    out = jnp.full(s.shape, default, F32)
    for r in reversed(range(thr_ref.shape[0])):
        out = jnp.where(s >= _row(thr_ref, r), value(r), out)
    return out


def _pair_word(x):
    bits = lax.bitcast_convert_type(x.astype(BF16).astype(F32), U32)
    return bits | (bits >> 16)


def _fold_keys_kernel(wq_ref, k1_ref, k2_ref, o_ref):
    wq = wq_ref[...]
    nt = (((1,), (1,)), ((), ()))
    a1 = lax.dot_general(k1_ref[...], wq[:, 0:D_HALF], nt, precision=lax.Precision.HIGHEST,
                         preferred_element_type=F32)
    a2 = lax.dot_general(k2_ref[...], wq[:, D_HALF:D_QUERY], nt, precision=lax.Precision.HIGHEST,
                         preferred_element_type=F32)
    o_ref[...] = pltpu.bitcast(jnp.concatenate([a1, a2], axis=0).astype(BF16), U32)


def _fold_keys_call(w_q, k1, k2, layer):
    d = D_MODEL
    return pl.pallas_call(
        _fold_keys_kernel,
        grid=(PEER_HEADS,),
        in_specs=[
            pl.BlockSpec((None, d, D_QUERY), lambda h: (layer, 0, h)),
            pl.BlockSpec((None, N_KEYS, D_HALF), lambda h: (layer, 0, 0)),
            pl.BlockSpec((None, N_KEYS, D_HALF), lambda h: (layer, 0, 0)),
        ],
        out_specs=pl.BlockSpec((N_KEYS, d), lambda h: (h, 0)),
        out_shape=jax.ShapeDtypeStruct((PEER_HEADS * N_KEYS, d), U32),
        name="fold_keys",
    )(w_q, k1, k2)


TAB_V1, TAB_N, TAB_ZINV = 0, PEER_TOPK, 2 * PEER_TOPK
TAB_ROWS = 3 * PEER_TOPK


def _first_key_factors(s1, tab_ref, tc):
    def row(i):
        return jnp.broadcast_to(tab_ref[i:i + 1, tc], s1.shape)
    n1 = jnp.zeros(s1.shape, F32)
    for r in reversed(range(PEER_TOPK)):
        n1 = jnp.where(s1 >= row(TAB_V1 + r), row(TAB_N + r), n1)
    return n1, jnp.exp(s1 - row(TAB_V1)) * row(TAB_ZINV)


def _route_select(c, s_ref, v2_tab, tab_ref, rb_ref, eb_ref):
    k = PEER_TOPK
    tc = slice(c * LANES, (c + 1) * LANES)
    v1 = _top_values(s_ref[0:N_KEYS, tc], k)
    v2 = _top_values(s_ref[N_KEYS:2 * N_KEYS, tc], k)
    v1_all = jnp.concatenate(v1, axis=0)
    v2_all = jnp.concatenate(v2, axis=0)
    pieces = [v1[r] + v2_all[0:k // (r + 1)] for r in range(k)]
    pad = -sum(p.shape[0] for p in pieces) % SUBLANES
    pieces.append(jnp.full((pad, LANES), -jnp.inf, F32))
    top = _top_values(jnp.concatenate(pieces, axis=0), k)
    thr = top[k - 1]
    z = sum(jnp.exp(t - top[0]) for t in top)
    n_sorted = jnp.zeros_like(v1_all)
    for r2 in range(k):
        n_sorted = jnp.where(v1_all + v2[r2] >= thr, float(r2 + 1), n_sorted)
    tab_ref[TAB_V1:TAB_V1 + k, tc] = v1_all
    tab_ref[TAB_N:TAB_N + k, tc] = n_sorted
    tab_ref[TAB_ZINV:TAB_ZINV + k, tc] = jnp.broadcast_to(1.0 / z, (k, LANES))
    v2_tab[...] = v2_all
    for g in range(N_KEYS // BF16_ROWS):
        rank2, eb = [], []
        for half in range(2):
            lo = N_KEYS + g * BF16_ROWS + half * SUBLANES
            s2 = s_ref[lo:lo + SUBLANES, tc]
            rank2.append(_threshold_lookup(s2, v2_tab, float, float(k)))
            eb.append(jnp.exp(s2 - _row(v2_tab, 0)))
        words = slice(g * SUBLANES, (g + 1) * SUBLANES)
        rb_ref[words, tc] = pltpu.bitcast(jnp.concatenate(rank2, axis=0).astype(BF16), U32)
        eb_ref[words, tc] = pltpu.bitcast(jnp.concatenate(eb, axis=0).astype(BF16), U32)


def _router_kernel(ht_ref, kw_ref, s_ref, tab_ref, rb_ref, eb_ref, v2_ref):
    s_ref[...] = jnp.dot(pltpu.bitcast(kw_ref[...], BF16), pltpu.bitcast(ht_ref[...], BF16),
                         preferred_element_type=F32)

    def head(hd, carry):
        row = pl.multiple_of(hd * 2 * N_KEYS, 2 * N_KEYS)
        scores = s_ref.at[pl.ds(row, 2 * N_KEYS)]
        for c in range(ROUTER_TOKENS // LANES):
            _route_select(c, scores, v2_ref.at[c], tab_ref.at[hd], rb_ref.at[hd], eb_ref.at[hd])
        return carry

    lax.fori_loop(0, PEER_HEADS, head, 0)


def _router_call(ht, kw):
    d = D_MODEL
    tr = ROUTER_TOKENS
    nt = N_TOKENS
    pair_spec = pl.BlockSpec((PEER_HEADS, N_KEYS // 2, tr), lambda i: (0, 0, i))
    return pl.pallas_call(
        _router_kernel,
        grid=(nt // tr,),
        in_specs=[
            pl.BlockSpec((d // 2, tr), lambda i: (0, i)),
            pl.BlockSpec((PEER_HEADS * N_KEYS, d), lambda i: (0, 0)),
        ],
        out_specs=[
            pl.BlockSpec((PEER_HEADS * 2 * N_KEYS, tr), lambda i: (0, i)),
            pl.BlockSpec((PEER_HEADS, TAB_ROWS, tr), lambda i: (0, 0, i)),
            pair_spec, pair_spec,
        ],
        out_shape=[
            jax.ShapeDtypeStruct((PEER_HEADS * 2 * N_KEYS, nt), F32),
            jax.ShapeDtypeStruct((PEER_HEADS, TAB_ROWS, nt), F32),
            jax.ShapeDtypeStruct((PEER_HEADS, N_KEYS // 2, nt), U32),
            jax.ShapeDtypeStruct((PEER_HEADS, N_KEYS // 2, nt), U32),
        ],
        scratch_shapes=[
            pltpu.VMEM((tr // LANES, PEER_TOPK, LANES), F32),
        ],
        compiler_params=pltpu.CompilerParams(
            dimension_semantics=("arbitrary",),
            vmem_limit_bytes=_vmem_limit(48 * 1024 * 1024)),
        name="peer_router",
    )(ht, kw)


def _gate_rows(e, act_ref, w_ref, nw_ref, eaw_ref, rb_ref, eb_ref):
    n_sub = N_KEYS // BF16_ROWS
    for c in range(EXPERT_TOKENS // LANES):
        tc = slice(c * LANES, (c + 1) * LANES)
        for k0 in range(0, n_sub, GATE_SUBS):
            subs = range(k0, k0 + GATE_SUBS)
            gates = [{kk: jnp.zeros((BF16_ROWS, LANES), BF16) for kk in subs} for _ in range(GATE_ROWS)]
            for hd in range(PEER_HEADS):
                n_rows, ea_rows = [], []
                for de in range(GATE_ROWS):
                    nw = jnp.broadcast_to(nw_ref[hd, e + de:e + de + 1, tc], (SUBLANES, LANES))
                    ew = jnp.broadcast_to(eaw_ref[hd, e + de:e + de + 1, tc], (SUBLANES, LANES))
                    n_rows.append(pltpu.bitcast(nw, BF16))
                    ea_rows.append(pltpu.bitcast(ew, BF16))
                for kk in subs:
                    ws = slice(kk * SUBLANES, (kk + 1) * SUBLANES)
                    rank2 = pltpu.bitcast(rb_ref[hd, ws, tc], BF16)
                    eb = pltpu.bitcast(eb_ref[hd, ws, tc], BF16)
                    for de in range(GATE_ROWS):
                        g = gates[de][kk]
                        gates[de][kk] = jnp.where(rank2 < n_rows[de], g + ea_rows[de] * eb, g)
            for de in range(GATE_ROWS):
                for kk in subs:
                    rows = slice((e + de) * N_KEYS + kk * BF16_ROWS, (e + de) * N_KEYS + (kk + 1) * BF16_ROWS)
                    w_ref[rows, tc] = _gelu(act_ref[rows, tc]).astype(BF16) * gates[de][kk]


def _spread_first_keys(s1_ref, tab_ref, nw_ref, eaw_ref):
    for hd in range(PEER_HEADS):
        for c in range(EXPERT_TOKENS // LANES):
            tc = slice(c * LANES, (c + 1) * LANES)
            for g in range(E1_PER_CHUNK // SUBLANES):
                rows = slice(g * SUBLANES, (g + 1) * SUBLANES)
                n1, ea = _first_key_factors(s1_ref[hd, rows, tc], tab_ref.at[hd], tc)
                nw_ref[hd, rows, tc] = _pair_word(n1)
                eaw_ref[hd, rows, tc] = _pair_word(ea)


def _expert_kernel(ht_ref, u_ref, vt_ref, s1_ref, tab_ref, rb_ref, eb_ref, x_ref, mod_ref, fg_ref,
                   out_ref, acc_ref, act0_ref, act1_ref, w0_ref, w1_ref, nw_ref, eaw_ref,
                   *, final_norm, n_steps, n_chunks):
    d = D_MODEL
    s = pl.program_id(0)
    sc = jnp.clip(s - 2, 0, n_steps - 1)
    jc = sc % n_chunks

    @pl.when(s == 0)
    def _():
        act1_ref[...] = jnp.zeros_like(act1_ref)
        w0_ref[...] = jnp.zeros_like(w0_ref)

    @pl.when(jc == 0)
    def _():
        acc_ref[...] = jnp.zeros_like(acc_ref)

    def step(act_a, act_b, w_b, w_c):
        n_groups = E1_PER_CHUNK // GATE_ROWS
        pieces_a = EXPERT_CHUNK // MXU_ROWS
        pieces_c = d // MXU_ROWS
        half = MXU_ROWS // 2
        for p in range(pieces_a):
            u = pltpu.bitcast(u_ref[p * half:(p + 1) * half, :], BF16)
            act_a[p * MXU_ROWS:(p + 1) * MXU_ROWS, :] = jnp.dot(
                u, pltpu.bitcast(ht_ref[...], BF16), preferred_element_type=F32)
            if p == 0:
                _spread_first_keys(s1_ref, tab_ref, nw_ref, eaw_ref)
            for q in range(p * n_groups // pieces_a, (p + 1) * n_groups // pieces_a):
                _gate_rows(GATE_ROWS * q, act_b, w_b, nw_ref, eaw_ref, rb_ref, eb_ref)
            if (p + 1) * pieces_c % pieces_a == 0:
                pc = (p + 1) * pieces_c // pieces_a - 1
                vt = pltpu.bitcast(vt_ref[pc * half:(pc + 1) * half, :], BF16)
                acc_ref[pc * MXU_ROWS:(pc + 1) * MXU_ROWS, :] += jnp.dot(
                    vt, w_c[...], preferred_element_type=F32)

    @pl.when(s % 2 == 0)
    def _():
        step(act0_ref, act1_ref, w1_ref, w0_ref)

    @pl.when(s % 2 == 1)
    def _():
        step(act1_ref, act0_ref, w0_ref, w1_ref)

    @pl.when((jc == n_chunks - 1) & (s >= 2))
    def _():
        mod = mod_ref[...]
        xn = x_ref[...] + mod[:, 5 * d:6 * d] * acc_ref[...].T
        if final_norm:
            ms = jnp.mean(xn * xn, axis=-1, keepdims=True)
            xn = xn * lax.rsqrt(ms + EPS) * fg_ref[...]
        out_ref[...] = xn


def _expert_call(ht, u_tab, vt_tab, scores, tab, rb, eb, x_flat, mod_l, final_g, final_norm):
    d = D_MODEL
    tb = EXPERT_TOKENS
    ec = EXPERT_CHUNK
    nt = N_TOKENS
    n_chunks = N_EXPERTS // ec
    n_steps = (nt // tb) * n_chunks

    def stage(lag):
        def split(s):
            t = jnp.clip(s - lag, 0, n_steps - 1)
            return t // n_chunks, t % n_chunks
        return split

    sa, sb, sc = stage(0), stage(1), stage(2)
    s1_spec = pl.BlockSpec((PEER_HEADS, E1_PER_CHUNK, tb), lambda s: (0, sb(s)[1], sb(s)[0]))
    tab_spec = pl.BlockSpec((PEER_HEADS, TAB_ROWS, tb), lambda s: (0, 0, sb(s)[0]))
    col_spec = pl.BlockSpec((PEER_HEADS, N_KEYS // 2, tb), lambda s: (0, 0, sb(s)[0]))
    return pl.pallas_call(
        functools.partial(_expert_kernel, final_norm=final_norm, n_steps=n_steps, n_chunks=n_chunks),
        grid=(n_steps + 2,),
        in_specs=[
            pl.BlockSpec((d // 2, tb), lambda s: (0, sa(s)[0])),
            pl.BlockSpec((ec // 2, d), lambda s: (sa(s)[1], 0)),
            pl.BlockSpec((d // 2, ec), lambda s: (0, sc(s)[1])),
            s1_spec, tab_spec, col_spec, col_spec,
            pl.BlockSpec((tb, d), lambda s: (sc(s)[0], 0)),
            pl.BlockSpec((None, 1, N_MOD * d), lambda s: (sc(s)[0] * tb // SEQ, 0, 0)),
            pl.BlockSpec((1, d), lambda s: (0, 0)),
        ],
        out_specs=pl.BlockSpec((tb, d), lambda s: (sc(s)[0], 0)),
        out_shape=jax.ShapeDtypeStruct((nt, d), F32),
        scratch_shapes=[
            pltpu.VMEM((d, tb), F32),
            pltpu.VMEM((ec, tb), F32),
            pltpu.VMEM((ec, tb), F32),
            pltpu.VMEM((ec, tb), BF16),
            pltpu.VMEM((ec, tb), BF16),
            pltpu.VMEM((PEER_HEADS, E1_PER_CHUNK, tb), U32),
            pltpu.VMEM((PEER_HEADS, E1_PER_CHUNK, tb), U32),
        ],
        compiler_params=pltpu.CompilerParams(
            dimension_semantics=("arbitrary",),
            vmem_limit_bytes=_vmem_limit(52 * 1024 * 1024)),
        name="peer_experts",
    )(ht, u_tab, vt_tab, scores.reshape(PEER_HEADS, 2 * N_KEYS, nt), tab, rb, eb, x_flat, mod_l, final_g)


def kernel(x, c, norm1_g, norm2_g, w_mod, b_mod, conv_w_in, conv_w, conv_w_out, pool_w_in, pool_w_grp,
           pool_scale, pool_w_out, peer_w_q, peer_k1, peer_k2, peer_u, peer_v, final_g):
    d = D_MODEL
    mod = _mod_call(c, w_mod, b_mod).reshape(DEPTH, BATCH, 1, N_MOD * d)
    fg = final_g.reshape(1, d)
    for i in range(DEPTH):
        g1 = norm1_g[i].reshape(1, d)
        g2 = norm2_g[i].reshape(1, d)
        jm = i // 2
        if i % 2 == 0:
            x, ht = _conv_call(x, mod[i], g1, g2, conv_w_in[jm].astype(BF16), conv_w[jm],
                               conv_w_out[jm].astype(BF16))
        else:
            x, ht = _pool_call(x, mod[i], g1, g2, pool_w_in[jm].astype(BF16), pool_w_grp[jm].astype(BF16),
                               pool_scale[jm].reshape(1, d), pool_w_out[jm].astype(BF16))
        scores, tab, rb, eb = _router_call(ht, _fold_keys_call(peer_w_q, peer_k1, peer_k2, i))
        x_flat = _expert_call(ht, _pack_rows_call(peer_u, i), _pack_cols_call(peer_v, i), scores, tab, rb, eb,
                              x.reshape(N_TOKENS, d), mod[i], fg, final_norm=(i == DEPTH - 1))
        x = x_flat.reshape(BATCH, SEQ, d)
    return x
```

```python
import functools
import math

import jax
import jax.numpy as jnp
from jax import lax
from jax.experimental import pallas as pl
from jax.experimental.pallas import tpu as pltpu

D_MODEL = 1024
BATCH = 8
SEQ = 2048
DEPTH = 2
CONV_WIDTH = 3
POOL_WINDOWS = (2, 4, 8, 16)
N_POOL_GROUPS = len(POOL_WINDOWS)
POOL_GROUP_W = D_MODEL // N_POOL_GROUPS
N_KEYS = 128
N_EXPERTS = N_KEYS * N_KEYS
PEER_HEADS = 8
PEER_TOPK = 16
D_QUERY = 256
D_HALF = D_QUERY // 2
N_MOD = 6
EPS = 1e-6
N_TOKENS = BATCH * SEQ

V7X_VMEM_BYTES = 64 * 1024 * 1024
LANES = 128
SUBLANES = 8
BF16_ROWS = 2 * SUBLANES

MIX_TOKENS = 512
ROUTER_TOKENS = 512
EXPERT_TOKENS = 512
EXPERT_CHUNK = 2048
E1_PER_CHUNK = EXPERT_CHUNK // N_KEYS
MXU_ROWS = 512
GATE_ROWS = 4
GATE_SUBS = 4
POOL_HALO = 16

F32 = jnp.float32
BF16 = jnp.bfloat16
U32 = jnp.uint32


def _vmem_limit(nbytes):
    return int(min(nbytes, V7X_VMEM_BYTES - 8 * 1024 * 1024))


def _norm_mod(x, g, scale, shift):
    ms = jnp.mean(x * x, axis=-1, keepdims=True)
    y = x * lax.rsqrt(ms + EPS)
    return (y * g) * (1.0 + scale) + shift


def _gelu(x):
    return 0.5 * x * (1.0 + lax.erf(x * (1.0 / math.sqrt(2.0))))


def _mod_kernel(c_ref, w_ref, b_ref, out_ref):
    c = c_ref[...]
    c_act = c * jax.nn.sigmoid(c)
    out_ref[...] = jnp.dot(c_act, w_ref[...], preferred_element_type=F32,
                           precision=lax.Precision.HIGHEST) + b_ref[...]


def _mod_call(c, w_mod, b_mod):
    d = D_MODEL
    return pl.pallas_call(
        _mod_kernel,
        grid=(DEPTH, N_MOD),
        in_specs=[
            pl.BlockSpec((BATCH, d), lambda l, n: (0, 0)),
            pl.BlockSpec((None, d, d), lambda l, n: (l, 0, n)),
            pl.BlockSpec((None, 1, d), lambda l, n: (l, 0, n)),
        ],
        out_specs=pl.BlockSpec((None, BATCH, d), lambda l, n: (l, 0, n)),
        out_shape=jax.ShapeDtypeStruct((DEPTH, BATCH, N_MOD * d), F32),
        name="adaln_mod",
    )(c, w_mod, b_mod.reshape(DEPTH, 1, N_MOD * d))


def _mixer_outputs(x_new, mod, g2_ref, out_ref, ht_ref):
    d = D_MODEL
    out_ref[...] = x_new
    h = _norm_mod(x_new, g2_ref[...], mod[:, 4 * d:5 * d], mod[:, 3 * d:4 * d])
    ht_ref[...] = pltpu.bitcast(h.T.astype(BF16), U32)


def _mixer_out_specs():
    d = D_MODEL
    ts = MIX_TOKENS
    specs = [pl.BlockSpec((None, ts, d), lambda b, s: (b, s, 0)),
             pl.BlockSpec((d // 2, ts), lambda b, s: (0, b * (SEQ // ts) + s))]
    shapes = [jax.ShapeDtypeStruct((BATCH, SEQ, d), F32),
              jax.ShapeDtypeStruct((d // 2, N_TOKENS), U32)]
    return specs, shapes


def _conv_kernel(x_ref, mod_ref, g_ref, g2_ref, w_in_ref, cw_ref, w_out_ref, out_ref, ht_ref, tail_ref):
    d = D_MODEL
    ts = MIX_TOKENS

    @pl.when(pl.program_id(1) == 0)
    def _():
        tail_ref[...] = jnp.zeros_like(tail_ref)

    x = x_ref[...]
    mod = mod_ref[...]
    h = _norm_mod(x, g_ref[...], mod[:, d:2 * d], mod[:, 0:d]).astype(BF16)
    proj = jnp.dot(h, w_in_ref[...], preferred_element_type=F32)
    b_gate = proj[:, 0:d]
    v = proj[:, d:2 * d] * proj[:, 2 * d:3 * d]
    tail = tail_ref[...]
    p1 = tail[SUBLANES - 1:SUBLANES]
    p2 = tail[SUBLANES - 2:SUBLANES - 1]
    rows = lax.broadcasted_iota(jnp.int32, (ts, d), 0)
    vm1 = jnp.where(rows >= 1, pltpu.roll(v, 1, 0), p1)
    vm2 = jnp.where(rows >= 2, pltpu.roll(v, 2, 0), jnp.where(rows == 1, p1, p2))
    cw = cw_ref[...]
    conv = cw[0:1] * vm2 + cw[1:2] * vm1 + cw[2:3] * v
    tail_ref[...] = v[ts - SUBLANES:ts]
    y = jnp.dot((b_gate * conv).astype(BF16), w_out_ref[...], preferred_element_type=F32)
    _mixer_outputs(x + mod[:, 2 * d:3 * d] * y, mod, g2_ref, out_ref, ht_ref)


def _conv_call(x, mod_l, g, g2, w_in, cw, w_out):
    d = D_MODEL
    ts = MIX_TOKENS
    out_specs, out_shape = _mixer_out_specs()
    return pl.pallas_call(
        _conv_kernel,
        grid=(BATCH, SEQ // ts),
        in_specs=[
            pl.BlockSpec((None, ts, d), lambda b, s: (b, s, 0)),
            pl.BlockSpec((None, 1, N_MOD * d), lambda b, s: (b, 0, 0)),
            pl.BlockSpec((1, d), lambda b, s: (0, 0)),
            pl.BlockSpec((1, d), lambda b, s: (0, 0)),
            pl.BlockSpec((d, 3 * d), lambda b, s: (0, 0)),
            pl.BlockSpec((CONV_WIDTH, d), lambda b, s: (0, 0)),
            pl.BlockSpec((d, d), lambda b, s: (0, 0)),
        ],
        out_specs=out_specs,
        out_shape=out_shape,
        scratch_shapes=[pltpu.VMEM((SUBLANES, d), F32)],
        compiler_params=pltpu.CompilerParams(
            dimension_semantics=("arbitrary", "arbitrary"),
            vmem_limit_bytes=_vmem_limit(48 * 1024 * 1024)),
        name="conv_mixer",
    )(x, mod_l, g, g2, w_in, cw, w_out)


def _pool_kernel(x_ref, mod_ref, g_ref, g2_ref, w_in_ref, w_grp_ref, scale_ref, w_out_ref,
                 out_ref, ht_ref, halo_ref):
    d = D_MODEL
    ts = MIX_TOKENS
    gw = POOL_GROUP_W
    s_blk = pl.program_id(1)

    @pl.when(s_blk == 0)
    def _():
        halo_ref[...] = jnp.zeros_like(halo_ref)

    x = x_ref[...]
    mod = mod_ref[...]
    h = _norm_mod(x, g_ref[...], mod[:, d:2 * d], mod[:, 0:d]).astype(BF16)
    u = jnp.dot(h, w_in_ref[...], preferred_element_type=F32)
    ext = jnp.concatenate([halo_ref[...], u], axis=0)
    halo_ref[...] = u[ts - POOL_HALO:ts]
    t_pos = s_blk * ts + lax.broadcasted_iota(jnp.int32, (ts, gw), 0)
    zs = []
    for gi, win in enumerate(POOL_WINDOWS):
        acc = ext[:, gi * gw:(gi + 1) * gw]
        span = 1
        while span < win:
            acc = acc + pltpu.roll(acc, span, 0)
            span *= 2
        wsum = acc[POOL_HALO:POOL_HALO + ts]
        cnt = jnp.minimum(t_pos + 1, win).astype(F32)
        pooled = wsum / cnt - u[:, gi * gw:(gi + 1) * gw]
        zs.append(jnp.dot(pooled.astype(BF16), w_grp_ref[gi], preferred_element_type=F32))
    z = jnp.concatenate(zs, axis=1) * scale_ref[...]
    y = jnp.dot(z.astype(BF16), w_out_ref[...], preferred_element_type=F32)
    _mixer_outputs(x + mod[:, 2 * d:3 * d] * y, mod, g2_ref, out_ref, ht_ref)


def _pool_call(x, mod_l, g, g2, w_in, w_grp, scale, w_out):
    d = D_MODEL
    ts = MIX_TOKENS
    gw = POOL_GROUP_W
    out_specs, out_shape = _mixer_out_specs()
    return pl.pallas_call(
        _pool_kernel,
        grid=(BATCH, SEQ // ts),
        in_specs=[
            pl.BlockSpec((None, ts, d), lambda b, s: (b, s, 0)),
            pl.BlockSpec((None, 1, N_MOD * d), lambda b, s: (b, 0, 0)),
            pl.BlockSpec((1, d), lambda b, s: (0, 0)),
            pl.BlockSpec((1, d), lambda b, s: (0, 0)),
            pl.BlockSpec((d, d), lambda b, s: (0, 0)),
            pl.BlockSpec((N_POOL_GROUPS, gw, gw), lambda b, s: (0, 0, 0)),
            pl.BlockSpec((1, d), lambda b, s: (0, 0)),
            pl.BlockSpec((d, d), lambda b, s: (0, 0)),
        ],
        out_specs=out_specs,
        out_shape=out_shape,
        scratch_shapes=[pltpu.VMEM((POOL_HALO, d), F32)],
        compiler_params=pltpu.CompilerParams(
            dimension_semantics=("arbitrary", "arbitrary"),
            vmem_limit_bytes=_vmem_limit(48 * 1024 * 1024)),
        name="pool_mixer",
    )(x, mod_l, g, g2, w_in, w_grp, scale, w_out)


def _sorting_network(n):
    pairs = []
    p = 1
    while p < n:
        k = p
        while k >= 1:
            for j in range(k % p, n - k, 2 * k):
                for i in range(min(k, n - j - k)):
                    if (i + j) // (2 * p) == (i + j + k) // (2 * p):
                        pairs.append((i + j, i + j + k))
            k //= 2
        p *= 2
    return pairs


def _top_values(s, k):
    n, t = s.shape
    groups = [s[g * SUBLANES:(g + 1) * SUBLANES] for g in range(n // SUBLANES)]
    n_net = 1
    while n_net < len(groups):
        n_net *= 2
    groups += [jnp.full((SUBLANES, t), -jnp.inf, F32)] * (n_net - len(groups))
    for i, j in _sorting_network(n_net):
        groups[i], groups[j] = jnp.maximum(groups[i], groups[j]), jnp.minimum(groups[i], groups[j])
    depth = min(len(groups), k)
    lists = groups[:depth]
    rows = []
    for r in range(k):
        m = jnp.max(lists[0], axis=0, keepdims=True)
        rows.append(m)
        live = min(depth, k - r - 1)
        hit = lists[0] >= m
        for j in range(live):
            below = lists[j + 1] if j + 1 < depth else -jnp.inf
            lists[j] = jnp.where(hit, below, lists[j])
    return rows


def _row(ref, i):
    return jnp.broadcast_to(ref[i:i + 1, :], (SUBLANES, ref.shape[-1]))


def _threshold_lookup(s, thr_ref, value, default):
    out = jnp.full(s.shape, default, F32)
    for r in reversed(range(thr_ref.shape[0])):
        out = jnp.where(s >= _row(thr_ref, r), value(r), out)
    return out


def _pair_word(x):
    bits = lax.bitcast_convert_type(x.astype(BF16).astype(F32), U32)
    return bits | (bits >> 16)


def _fold_keys_kernel(wq_ref, k1_ref, k2_ref, o_ref):
    wq = wq_ref[...]
    nt = (((1,), (1,)), ((), ()))
    a1 = lax.dot_general(k1_ref[...], wq[:, 0:D_HALF], nt, precision=lax.Precision.HIGHEST,
                         preferred_element_type=F32)
    a2 = lax.dot_general(k2_ref[...], wq[:, D_HALF:D_QUERY], nt, precision=lax.Precision.HIGHEST,
                         preferred_element_type=F32)
    o_ref[...] = pltpu.bitcast(jnp.concatenate([a1, a2], axis=0).astype(BF16), U32)


def _fold_keys_call(w_q, k1, k2, layer):
    d = D_MODEL
    return pl.pallas_call(
        _fold_keys_kernel,
        grid=(PEER_HEADS,),
        in_specs=[
            pl.BlockSpec((None, d, D_QUERY), lambda h: (layer, 0, h)),
            pl.BlockSpec((None, N_KEYS, D_HALF), lambda h: (layer, 0, 0)),
            pl.BlockSpec((None, N_KEYS, D_HALF), lambda h: (layer, 0, 0)),
        ],
        out_specs=pl.BlockSpec((N_KEYS, d), lambda h: (h, 0)),
        out_shape=jax.ShapeDtypeStruct((PEER_HEADS * N_KEYS, d), U32),
        name="fold_keys",
    )(w_q, k1, k2)


TAB_V1, TAB_N, TAB_ZINV = 0, PEER_TOPK, 2 * PEER_TOPK
TAB_ROWS = 3 * PEER_TOPK


def _first_key_factors(s1, tab_ref, tc):
    def row(i):
        return jnp.broadcast_to(tab_ref[i:i + 1, tc], s1.shape)
    n1 = jnp.zeros(s1.shape, F32)
    for r in reversed(range(PEER_TOPK)):
        n1 = jnp.where(s1 >= row(TAB_V1 + r), row(TAB_N + r), n1)
    return n1, jnp.exp(s1 - row(TAB_V1)) * row(TAB_ZINV)


def _route_select(c, s_ref, v2_tab, tab_ref, rb_ref, eb_ref):
    k = PEER_TOPK
    tc = slice(c * LANES, (c + 1) * LANES)
    v1 = _top_values(s_ref[0:N_KEYS, tc], k)
    v2 = _top_values(s_ref[N_KEYS:2 * N_KEYS, tc], k)
    v1_all = jnp.concatenate(v1, axis=0)
    v2_all = jnp.concatenate(v2, axis=0)
    pieces = [v1[r] + v2_all[0:k // (r + 1)] for r in range(k)]
    pad = -sum(p.shape[0] for p in pieces) % SUBLANES
    pieces.append(jnp.full((pad, LANES), -jnp.inf, F32))
    top = _top_values(jnp.concatenate(pieces, axis=0), k)
    thr = top[k - 1]
    z = sum(jnp.exp(t - top[0]) for t in top)
    n_sorted = jnp.zeros_like(v1_all)
    for r2 in range(k):
        n_sorted = jnp.where(v1_all + v2[r2] >= thr, float(r2 + 1), n_sorted)
    tab_ref[TAB_V1:TAB_V1 + k, tc] = v1_all
    tab_ref[TAB_N:TAB_N + k, tc] = n_sorted
    tab_ref[TAB_ZINV:TAB_ZINV + k, tc] = jnp.broadcast_to(1.0 / z, (k, LANES))
    v2_tab[...] = v2_all
    for g in range(N_KEYS // BF16_ROWS):
        rank2, eb = [], []
        for half in range(2):
            lo = N_KEYS + g * BF16_ROWS + half * SUBLANES
            s2 = s_ref[lo:lo + SUBLANES, tc]
            rank2.append(_threshold_lookup(s2, v2_tab, float, float(k)))
            eb.append(jnp.exp(s2 - _row(v2_tab, 0)))
        words = slice(g * SUBLANES, (g + 1) * SUBLANES)
        rb_ref[words, tc] = pltpu.bitcast(jnp.concatenate(rank2, axis=0).astype(BF16), U32)
        eb_ref[words, tc] = pltpu.bitcast(jnp.concatenate(eb, axis=0).astype(BF16), U32)


def _router_kernel(ht_ref, kw_ref, u_ref, v_ref, s_ref, tab_ref, rb_ref, eb_ref, up_ref, vtp_ref, v2_ref):
    s_ref[...] = jnp.dot(pltpu.bitcast(kw_ref[...], BF16), pltpu.bitcast(ht_ref[...], BF16),
                         preferred_element_type=F32)
    up_ref[...] = pltpu.bitcast(u_ref[...].astype(BF16), U32)
    vtp_ref[...] = pltpu.bitcast(v_ref[...].T.astype(BF16), U32)

    def head(hd, carry):
        row = pl.multiple_of(hd * 2 * N_KEYS, 2 * N_KEYS)
        scores = s_ref.at[pl.ds(row, 2 * N_KEYS)]
        for c in range(ROUTER_TOKENS // LANES):
            _route_select(c, scores, v2_ref.at[c], tab_ref.at[hd], rb_ref.at[hd], eb_ref.at[hd])
        return carry

    lax.fori_loop(0, PEER_HEADS, head, 0)


def _router_call(ht, kw, u_tab, v_tab, layer):
    d = D_MODEL
    tr = ROUTER_TOKENS
    nt = N_TOKENS
    n_steps = nt // tr
    rows = N_EXPERTS // n_steps
    assert rows * n_steps == N_EXPERTS and rows % BF16_ROWS == 0
    pair_spec = pl.BlockSpec((PEER_HEADS, N_KEYS // 2, tr), lambda i: (0, 0, i))
    table_spec = pl.BlockSpec((None, rows, d), lambda i: (layer, i, 0))
    return pl.pallas_call(
        _router_kernel,
        grid=(n_steps,),
        in_specs=[
            pl.BlockSpec((d // 2, tr), lambda i: (0, i)),
            pl.BlockSpec((PEER_HEADS * N_KEYS, d), lambda i: (0, 0)),
            table_spec, table_spec,
        ],
        out_specs=[
            pl.BlockSpec((PEER_HEADS * 2 * N_KEYS, tr), lambda i: (0, i)),
            pl.BlockSpec((PEER_HEADS, TAB_ROWS, tr), lambda i: (0, 0, i)),
            pair_spec, pair_spec,
            pl.BlockSpec((rows // 2, d), lambda i: (i, 0)),
            pl.BlockSpec((d // 2, rows), lambda i: (0, i)),
        ],
        out_shape=[
            jax.ShapeDtypeStruct((PEER_HEADS * 2 * N_KEYS, nt), F32),
            jax.ShapeDtypeStruct((PEER_HEADS, TAB_ROWS, nt), F32),
            jax.ShapeDtypeStruct((PEER_HEADS, N_KEYS // 2, nt), U32),
            jax.ShapeDtypeStruct((PEER_HEADS, N_KEYS // 2, nt), U32),
            jax.ShapeDtypeStruct((N_EXPERTS // 2, d), U32),
            jax.ShapeDtypeStruct((d // 2, N_EXPERTS), U32),
        ],
        scratch_shapes=[
            pltpu.VMEM((tr // LANES, PEER_TOPK, LANES), F32),
        ],
        compiler_params=pltpu.CompilerParams(
            dimension_semantics=("arbitrary",),
            vmem_limit_bytes=_vmem_limit(48 * 1024 * 1024)),
        name="peer_router",
    )(ht, kw, u_tab, v_tab)


def _gate_rows(e, act_ref, w_ref, nw_ref, eaw_ref, rb_ref, eb_ref):
    n_sub = N_KEYS // BF16_ROWS
    for c in range(EXPERT_TOKENS // LANES):
        tc = slice(c * LANES, (c + 1) * LANES)
        for k0 in range(0, n_sub, GATE_SUBS):
            subs = range(k0, k0 + GATE_SUBS)
            gates = [{kk: jnp.zeros((BF16_ROWS, LANES), BF16) for kk in subs} for _ in range(GATE_ROWS)]
            for hd in range(PEER_HEADS):
                n_rows, ea_rows = [], []
                for de in range(GATE_ROWS):
                    nw = jnp.broadcast_to(nw_ref[hd, e + de:e + de + 1, tc], (SUBLANES, LANES))
                    ew = jnp.broadcast_to(eaw_ref[hd, e + de:e + de + 1, tc], (SUBLANES, LANES))
                    n_rows.append(pltpu.bitcast(nw, BF16))
                    ea_rows.append(pltpu.bitcast(ew, BF16))
                for kk in subs:
                    ws = slice(kk * SUBLANES, (kk + 1) * SUBLANES)
                    rank2 = pltpu.bitcast(rb_ref[hd, ws, tc], BF16)
                    eb = pltpu.bitcast(eb_ref[hd, ws, tc], BF16)
                    for de in range(GATE_ROWS):
                        g = gates[de][kk]
                        gates[de][kk] = jnp.where(rank2 < n_rows[de], g + ea_rows[de] * eb, g)
            for de in range(GATE_ROWS):
                for kk in subs:
                    rows = slice((e + de) * N_KEYS + kk * BF16_ROWS, (e + de) * N_KEYS + (kk + 1) * BF16_ROWS)
                    w_ref[rows, tc] = _gelu(act_ref[rows, tc]).astype(BF16) * gates[de][kk]


def _spread_first_keys(s1_ref, tab_ref, nw_ref, eaw_ref):
    for hd in range(PEER_HEADS):
        for c in range(EXPERT_TOKENS // LANES):
            tc = slice(c * LANES, (c + 1) * LANES)
            for g in range(E1_PER_CHUNK // SUBLANES):
                rows = slice(g * SUBLANES, (g + 1) * SUBLANES)
                n1, ea = _first_key_factors(s1_ref[hd, rows, tc], tab_ref.at[hd], tc)
                nw_ref[hd, rows, tc] = _pair_word(n1)
                eaw_ref[hd, rows, tc] = _pair_word(ea)


def _expert_kernel(ht_ref, u_ref, vt_ref, s1_ref, tab_ref, rb_ref, eb_ref, x_ref, mod_ref, fg_ref,
                   out_ref, acc_ref, act0_ref, act1_ref, w0_ref, w1_ref, nw_ref, eaw_ref,
                   *, final_norm, n_steps, n_chunks):
    d = D_MODEL
    s = pl.program_id(0)
    sc = jnp.clip(s - 2, 0, n_steps - 1)
    jc = sc % n_chunks

    @pl.when(s == 0)
    def _():
        act1_ref[...] = jnp.zeros_like(act1_ref)
        w0_ref[...] = jnp.zeros_like(w0_ref)

    @pl.when(jc == 0)
    def _():
        acc_ref[...] = jnp.zeros_like(acc_ref)

    def step(act_a, act_b, w_b, w_c):
        n_groups = E1_PER_CHUNK // GATE_ROWS
        pieces_a = EXPERT_CHUNK // MXU_ROWS
        pieces_c = d // MXU_ROWS
        half = MXU_ROWS // 2
        for p in range(pieces_a):
            u = pltpu.bitcast(u_ref[p * half:(p + 1) * half, :], BF16)
            act_a[p * MXU_ROWS:(p + 1) * MXU_ROWS, :] = jnp.dot(
                u, pltpu.bitcast(ht_ref[...], BF16), preferred_element_type=F32)
            if p == 0:
                _spread_first_keys(s1_ref, tab_ref, nw_ref, eaw_ref)
            for q in range(p * n_groups // pieces_a, (p + 1) * n_groups // pieces_a):
                _gate_rows(GATE_ROWS * q, act_b, w_b, nw_ref, eaw_ref, rb_ref, eb_ref)
            if (p + 1) * pieces_c % pieces_a == 0:
                pc = (p + 1) * pieces_c // pieces_a - 1
                vt = pltpu.bitcast(vt_ref[pc * half:(pc + 1) * half, :], BF16)
                acc_ref[pc * MXU_ROWS:(pc + 1) * MXU_ROWS, :] += jnp.dot(
                    vt, w_c[...], preferred_element_type=F32)

    @pl.when(s % 2 == 0)
    def _():
        step(act0_ref, act1_ref, w1_ref, w0_ref)

    @pl.when(s % 2 == 1)
    def _():
        step(act1_ref, act0_ref, w0_ref, w1_ref)

    @pl.when((jc == n_chunks - 1) & (s >= 2))
    def _():
        mod = mod_ref[...]
        xn = x_ref[...] + mod[:, 5 * d:6 * d] * acc_ref[...].T
        if final_norm:
            ms = jnp.mean(xn * xn, axis=-1, keepdims=True)
            xn = xn * lax.rsqrt(ms + EPS) * fg_ref[...]
        out_ref[...] = xn


def _expert_call(ht, u_tab, vt_tab, scores, tab, rb, eb, x_flat, mod_l, final_g, final_norm):
    d = D_MODEL
    tb = EXPERT_TOKENS
    ec = EXPERT_CHUNK
    nt = N_TOKENS
    n_chunks = N_EXPERTS // ec
    n_steps = (nt // tb) * n_chunks

    def stage(lag):
        def split(s):
            t = jnp.clip(s - lag, 0, n_steps - 1)
            return t // n_chunks, t % n_chunks
        return split

    sa, sb, sc = stage(0), stage(1), stage(2)
    s1_spec = pl.BlockSpec((PEER_HEADS, E1_PER_CHUNK, tb), lambda s: (0, sb(s)[1], sb(s)[0]))
    tab_spec = pl.BlockSpec((PEER_HEADS, TAB_ROWS, tb), lambda s: (0, 0, sb(s)[0]))
    col_spec = pl.BlockSpec((PEER_HEADS, N_KEYS // 2, tb), lambda s: (0, 0, sb(s)[0]))
    return pl.pallas_call(
        functools.partial(_expert_kernel, final_norm=final_norm, n_steps=n_steps, n_chunks=n_chunks),
        grid=(n_steps + 2,),
        in_specs=[
            pl.BlockSpec((d // 2, tb), lambda s: (0, sa(s)[0])),
            pl.BlockSpec((ec // 2, d), lambda s: (sa(s)[1], 0)),
            pl.BlockSpec((d // 2, ec), lambda s: (0, sc(s)[1])),
            s1_spec, tab_spec, col_spec, col_spec,
            pl.BlockSpec((tb, d), lambda s: (sc(s)[0], 0)),
            pl.BlockSpec((None, 1, N_MOD * d), lambda s: (sc(s)[0] * tb // SEQ, 0, 0)),
            pl.BlockSpec((1, d), lambda s: (0, 0)),
        ],
        out_specs=pl.BlockSpec((tb, d), lambda s: (sc(s)[0], 0)),
        out_shape=jax.ShapeDtypeStruct((nt, d), F32),
        scratch_shapes=[
            pltpu.VMEM((d, tb), F32),
            pltpu.VMEM((ec, tb), F32),
            pltpu.VMEM((ec, tb), F32),
            pltpu.VMEM((ec, tb), BF16),
            pltpu.VMEM((ec, tb), BF16),
            pltpu.VMEM((PEER_HEADS, E1_PER_CHUNK, tb), U32),
            pltpu.VMEM((PEER_HEADS, E1_PER_CHUNK, tb), U32),
        ],
        compiler_params=pltpu.CompilerParams(
            dimension_semantics=("arbitrary",),
            vmem_limit_bytes=_vmem_limit(52 * 1024 * 1024)),
        name="peer_experts",
    )(ht, u_tab, vt_tab, scores.reshape(PEER_HEADS, 2 * N_KEYS, nt), tab, rb, eb, x_flat, mod_l, final_g)


def kernel(x, c, norm1_g, norm2_g, w_mod, b_mod, conv_w_in, conv_w, conv_w_out, pool_w_in, pool_w_grp,
           pool_scale, pool_w_out, peer_w_q, peer_k1, peer_k2, peer_u, peer_v, final_g):
    d = D_MODEL
    mod = _mod_call(c, w_mod, b_mod).reshape(DEPTH, BATCH, 1, N_MOD * d)
    fg = final_g.reshape(1, d)
    for i in range(DEPTH):
        g1 = norm1_g[i].reshape(1, d)
        g2 = norm2_g[i].reshape(1, d)
        jm = i // 2
        if i % 2 == 0:
            x, ht = _conv_call(x, mod[i], g1, g2, conv_w_in[jm].astype(BF16), conv_w[jm],
                               conv_w_out[jm].astype(BF16))
        else:
            x, ht = _pool_call(x, mod[i], g1, g2, pool_w_in[jm].astype(BF16), pool_w_grp[jm].astype(BF16),
                               pool_scale[jm].reshape(1, d), pool_w_out[jm].astype(BF16))
        scores, tab, rb, eb, u_pairs, vt_pairs = _router_call(
            ht, _fold_keys_call(peer_w_q, peer_k1, peer_k2, i), peer_u, peer_v, i)
        x_flat = _expert_call(ht, u_pairs, vt_pairs, scores, tab, rb, eb,
                              x.reshape(N_TOKENS, d), mod[i], fg, final_norm=(i == DEPTH - 1))
        x = x_flat.reshape(BATCH, SEQ, d)
    return x
```

```python
import functools
import math

import jax
import jax.numpy as jnp
from jax import lax
from jax.experimental import pallas as pl
from jax.experimental.pallas import tpu as pltpu

D_MODEL = 1024
BATCH = 8
SEQ = 2048
DEPTH = 2
CONV_WIDTH = 3
POOL_WINDOWS = (2, 4, 8, 16)
N_POOL_GROUPS = len(POOL_WINDOWS)
POOL_GROUP_W = D_MODEL // N_POOL_GROUPS
N_KEYS = 128
N_EXPERTS = N_KEYS * N_KEYS
PEER_HEADS = 8
PEER_TOPK = 16
D_QUERY = 256
D_HALF = D_QUERY // 2
N_MOD = 6
EPS = 1e-6
N_TOKENS = BATCH * SEQ

V7X_VMEM_BYTES = 64 * 1024 * 1024
LANES = 128
SUBLANES = 8
BF16_ROWS = 2 * SUBLANES

MIX_TOKENS = 512
ROUTER_TOKENS = 512
EXPERT_TOKENS = 512
EXPERT_CHUNK = 2048
E1_PER_CHUNK = EXPERT_CHUNK // N_KEYS
MXU_ROWS = 256
GATE_ROWS = 4
GATE_SUBS = 4
POOL_HALO = 16

F32 = jnp.float32
BF16 = jnp.bfloat16
U32 = jnp.uint32


def _vmem_limit(nbytes):
    return int(min(nbytes, V7X_VMEM_BYTES - 8 * 1024 * 1024))


def _norm_mod(x, g, scale, shift):
    ms = jnp.mean(x * x, axis=-1, keepdims=True)
    y = x * lax.rsqrt(ms + EPS)
    return (y * g) * (1.0 + scale) + shift


def _gelu(x):
    return 0.5 * x * (1.0 + lax.erf(x * (1.0 / math.sqrt(2.0))))


def _mod_kernel(c_ref, w_ref, b_ref, out_ref):
    c = c_ref[...]
    c_act = c * jax.nn.sigmoid(c)
    out_ref[...] = jnp.dot(c_act, w_ref[...], preferred_element_type=F32,
                           precision=lax.Precision.HIGHEST) + b_ref[...]


def _mod_call(c, w_mod, b_mod):
    d = D_MODEL
    return pl.pallas_call(
        _mod_kernel,
        grid=(DEPTH, N_MOD),
        in_specs=[
            pl.BlockSpec((BATCH, d), lambda l, n: (0, 0)),
            pl.BlockSpec((None, d, d), lambda l, n: (l, 0, n)),
            pl.BlockSpec((None, 1, d), lambda l, n: (l, 0, n)),
        ],
        out_specs=pl.BlockSpec((None, BATCH, d), lambda l, n: (l, 0, n)),
        out_shape=jax.ShapeDtypeStruct((DEPTH, BATCH, N_MOD * d), F32),
        name="adaln_mod",
    )(c, w_mod, b_mod.reshape(DEPTH, 1, N_MOD * d))


def _mixer_outputs(x_new, mod, g2_ref, out_ref, ht_ref):
    d = D_MODEL
    out_ref[...] = x_new
    h = _norm_mod(x_new, g2_ref[...], mod[:, 4 * d:5 * d], mod[:, 3 * d:4 * d])
    ht_ref[...] = pltpu.bitcast(h.T.astype(BF16), U32)


def _mixer_out_specs():
    d = D_MODEL
    ts = MIX_TOKENS
    specs = [pl.BlockSpec((None, ts, d), lambda b, s: (b, s, 0)),
             pl.BlockSpec((d // 2, ts), lambda b, s: (0, b * (SEQ // ts) + s))]
    shapes = [jax.ShapeDtypeStruct((BATCH, SEQ, d), F32),
              jax.ShapeDtypeStruct((d // 2, N_TOKENS), U32)]
    return specs, shapes


def _conv_kernel(x_ref, mod_ref, g_ref, g2_ref, w_in_ref, cw_ref, w_out_ref, out_ref, ht_ref, tail_ref):
    d = D_MODEL
    ts = MIX_TOKENS

    @pl.when(pl.program_id(1) == 0)
    def _():
        tail_ref[...] = jnp.zeros_like(tail_ref)

    x = x_ref[...]
    mod = mod_ref[...]
    h = _norm_mod(x, g_ref[...], mod[:, d:2 * d], mod[:, 0:d]).astype(BF16)
    proj = jnp.dot(h, w_in_ref[...], preferred_element_type=F32)
    b_gate = proj[:, 0:d]
    v = proj[:, d:2 * d] * proj[:, 2 * d:3 * d]
    tail = tail_ref[...]
    p1 = tail[SUBLANES - 1:SUBLANES]
    p2 = tail[SUBLANES - 2:SUBLANES - 1]
    rows = lax.broadcasted_iota(jnp.int32, (ts, d), 0)
    vm1 = jnp.where(rows >= 1, pltpu.roll(v, 1, 0), p1)
    vm2 = jnp.where(rows >= 2, pltpu.roll(v, 2, 0), jnp.where(rows == 1, p1, p2))
    cw = cw_ref[...]
    conv = cw[0:1] * vm2 + cw[1:2] * vm1 + cw[2:3] * v
    tail_ref[...] = v[ts - SUBLANES:ts]
    y = jnp.dot((b_gate * conv).astype(BF16), w_out_ref[...], preferred_element_type=F32)
    _mixer_outputs(x + mod[:, 2 * d:3 * d] * y, mod, g2_ref, out_ref, ht_ref)


def _conv_call(x, mod_l, g, g2, w_in, cw, w_out):
    d = D_MODEL
    ts = MIX_TOKENS
    out_specs, out_shape = _mixer_out_specs()
    return pl.pallas_call(
        _conv_kernel,
        grid=(BATCH, SEQ // ts),
        in_specs=[
            pl.BlockSpec((None, ts, d), lambda b, s: (b, s, 0)),
            pl.BlockSpec((None, 1, N_MOD * d), lambda b, s: (b, 0, 0)),
            pl.BlockSpec((1, d), lambda b, s: (0, 0)),
            pl.BlockSpec((1, d), lambda b, s: (0, 0)),
            pl.BlockSpec((d, 3 * d), lambda b, s: (0, 0)),
            pl.BlockSpec((CONV_WIDTH, d), lambda b, s: (0, 0)),
            pl.BlockSpec((d, d), lambda b, s: (0, 0)),
        ],
        out_specs=out_specs,
        out_shape=out_shape,
        scratch_shapes=[pltpu.VMEM((SUBLANES, d), F32)],
        compiler_params=pltpu.CompilerParams(
            dimension_semantics=("arbitrary", "arbitrary"),
            vmem_limit_bytes=_vmem_limit(48 * 1024 * 1024)),
        name="conv_mixer",
    )(x, mod_l, g, g2, w_in, cw, w_out)


def _pool_kernel(x_ref, mod_ref, g_ref, g2_ref, w_in_ref, w_grp_ref, scale_ref, w_out_ref,
                 out_ref, ht_ref, halo_ref):
    d = D_MODEL
    ts = MIX_TOKENS
    gw = POOL_GROUP_W
    s_blk = pl.program_id(1)

    @pl.when(s_blk == 0)
    def _():
        halo_ref[...] = jnp.zeros_like(halo_ref)

    x = x_ref[...]
    mod = mod_ref[...]
    h = _norm_mod(x, g_ref[...], mod[:, d:2 * d], mod[:, 0:d]).astype(BF16)
    u = jnp.dot(h, w_in_ref[...], preferred_element_type=F32)
    ext = jnp.concatenate([halo_ref[...], u], axis=0)
    halo_ref[...] = u[ts - POOL_HALO:ts]
    t_pos = s_blk * ts + lax.broadcasted_iota(jnp.int32, (ts, gw), 0)
    zs = []
    for gi, win in enumerate(POOL_WINDOWS):
        acc = ext[:, gi * gw:(gi + 1) * gw]
        span = 1
        while span < win:
            acc = acc + pltpu.roll(acc, span, 0)
            span *= 2
        wsum = acc[POOL_HALO:POOL_HALO + ts]
        cnt = jnp.minimum(t_pos + 1, win).astype(F32)
        pooled = wsum / cnt - u[:, gi * gw:(gi + 1) * gw]
        zs.append(jnp.dot(pooled.astype(BF16), w_grp_ref[gi], preferred_element_type=F32))
    z = jnp.concatenate(zs, axis=1) * scale_ref[...]
    y = jnp.dot(z.astype(BF16), w_out_ref[...], preferred_element_type=F32)
    _mixer_outputs(x + mod[:, 2 * d:3 * d] * y, mod, g2_ref, out_ref, ht_ref)


def _pool_call(x, mod_l, g, g2, w_in, w_grp, scale, w_out):
    d = D_MODEL
    ts = MIX_TOKENS
    gw = POOL_GROUP_W
    out_specs, out_shape = _mixer_out_specs()
    return pl.pallas_call(
        _pool_kernel,
        grid=(BATCH, SEQ // ts),
        in_specs=[
            pl.BlockSpec((None, ts, d), lambda b, s: (b, s, 0)),
            pl.BlockSpec((None, 1, N_MOD * d), lambda b, s: (b, 0, 0)),
            pl.BlockSpec((1, d), lambda b, s: (0, 0)),
            pl.BlockSpec((1, d), lambda b, s: (0, 0)),
            pl.BlockSpec((d, d), lambda b, s: (0, 0)),
            pl.BlockSpec((N_POOL_GROUPS, gw, gw), lambda b, s: (0, 0, 0)),
            pl.BlockSpec((1, d), lambda b, s: (0, 0)),
            pl.BlockSpec((d, d), lambda b, s: (0, 0)),
        ],
        out_specs=out_specs,
        out_shape=out_shape,
        scratch_shapes=[pltpu.VMEM((POOL_HALO, d), F32)],
        compiler_params=pltpu.CompilerParams(
            dimension_semantics=("arbitrary", "arbitrary"),
            vmem_limit_bytes=_vmem_limit(48 * 1024 * 1024)),
        name="pool_mixer",
    )(x, mod_l, g, g2, w_in, w_grp, scale, w_out)


def _sorting_network(n):
    pairs = []
    p = 1
    while p < n:
        k = p
        while k >= 1:
            for j in range(k % p, n - k, 2 * k):
                for i in range(min(k, n - j - k)):
                    if (i + j) // (2 * p) == (i + j + k) // (2 * p):
                        pairs.append((i + j, i + j + k))
            k //= 2
        p *= 2
    return pairs


def _top_values(s, k):
    n, t = s.shape
    groups = [s[g * SUBLANES:(g + 1) * SUBLANES] for g in range(n // SUBLANES)]
    n_net = 1
    while n_net < len(groups):
        n_net *= 2
    groups += [jnp.full((SUBLANES, t), -jnp.inf, F32)] * (n_net - len(groups))
    for i, j in _sorting_network(n_net):
        groups[i], groups[j] = jnp.maximum(groups[i], groups[j]), jnp.minimum(groups[i], groups[j])
    depth = min(len(groups), k)
    lists = groups[:depth]
    rows = []
    for r in range(k):
        m = jnp.max(lists[0], axis=0, keepdims=True)
        rows.append(m)
        live = min(depth, k - r - 1)
        hit = lists[0] >= m
        for j in range(live):
            below = lists[j + 1] if j + 1 < depth else -jnp.inf
            lists[j] = jnp.where(hit, below, lists[j])
    return rows


def _row(ref, i):
    return jnp.broadcast_to(ref[i:i + 1, :], (SUBLANES, ref.shape[-1]))


def _threshold_lookup(s, thr_ref, value, default):
    out = jnp.full(s.shape, default, F32)
    for r in reversed(range(thr_ref.shape[0])):
        out = jnp.where(s >= _row(thr_ref, r), value(r), out)
    return out


def _pair_word(x):
    bits = lax.bitcast_convert_type(x.astype(BF16).astype(F32), U32)
    return bits | (bits >> 16)


def _fold_keys_kernel(wq_ref, k1_ref, k2_ref, o_ref):
    wq = wq_ref[...]
    nt = (((1,), (1,)), ((), ()))
    a1 = lax.dot_general(k1_ref[...], wq[:, 0:D_HALF], nt, precision=lax.Precision.HIGHEST,
                         preferred_element_type=F32)
    a2 = lax.dot_general(k2_ref[...], wq[:, D_HALF:D_QUERY], nt, precision=lax.Precision.HIGHEST,
                         preferred_element_type=F32)
    o_ref[...] = pltpu.bitcast(jnp.concatenate([a1, a2], axis=0).astype(BF16), U32)


def _fold_keys_call(w_q, k1, k2, layer):
    d = D_MODEL
    return pl.pallas_call(
        _fold_keys_kernel,
        grid=(PEER_HEADS,),
        in_specs=[
            pl.BlockSpec((None, d, D_QUERY), lambda h: (layer, 0, h)),
            pl.BlockSpec((None, N_KEYS, D_HALF), lambda h: (layer, 0, 0)),
            pl.BlockSpec((None, N_KEYS, D_HALF), lambda h: (layer, 0, 0)),
        ],
        out_specs=pl.BlockSpec((N_KEYS, d), lambda h: (h, 0)),
        out_shape=jax.ShapeDtypeStruct((PEER_HEADS * N_KEYS, d), U32),
        name="fold_keys",
    )(w_q, k1, k2)


TAB_V1, TAB_N, TAB_ZINV = 0, PEER_TOPK, 2 * PEER_TOPK
TAB_ROWS = 3 * PEER_TOPK


def _first_key_factors(s1, tab_ref, tc):
    def row(i):
        return jnp.broadcast_to(tab_ref[i:i + 1, tc], s1.shape)
    n1 = jnp.zeros(s1.shape, F32)
    for r in reversed(range(PEER_TOPK)):
        n1 = jnp.where(s1 >= row(TAB_V1 + r), row(TAB_N + r), n1)
    return n1, jnp.exp(s1 - row(TAB_V1)) * row(TAB_ZINV)


def _route_select(c, s_ref, v2_tab, tab_ref, rb_ref, eb_ref):
    k = PEER_TOPK
    tc = slice(c * LANES, (c + 1) * LANES)
    v1 = _top_values(s_ref[0:N_KEYS, tc], k)
    v2 = _top_values(s_ref[N_KEYS:2 * N_KEYS, tc], k)
    v1_all = jnp.concatenate(v1, axis=0)
    v2_all = jnp.concatenate(v2, axis=0)
    pieces = [v1[r] + v2_all[0:k // (r + 1)] for r in range(k)]
    pad = -sum(p.shape[0] for p in pieces) % SUBLANES
    pieces.append(jnp.full((pad, LANES), -jnp.inf, F32))
    top = _top_values(jnp.concatenate(pieces, axis=0), k)
    thr = top[k - 1]
    z = sum(jnp.exp(t - top[0]) for t in top)
    n_sorted = jnp.zeros_like(v1_all)
    for r2 in range(k):
        n_sorted = jnp.where(v1_all + v2[r2] >= thr, float(r2 + 1), n_sorted)
    tab_ref[TAB_V1:TAB_V1 + k, tc] = v1_all
    tab_ref[TAB_N:TAB_N + k, tc] = n_sorted
    tab_ref[TAB_ZINV:TAB_ZINV + k, tc] = jnp.broadcast_to(1.0 / z, (k, LANES))
    v2_tab[...] = v2_all
    for g in range(N_KEYS // BF16_ROWS):
        rank2, eb = [], []
        for half in range(2):
            lo = N_KEYS + g * BF16_ROWS + half * SUBLANES
            s2 = s_ref[lo:lo + SUBLANES, tc]
            rank2.append(_threshold_lookup(s2, v2_tab, float, float(k)))
            eb.append(jnp.exp(s2 - _row(v2_tab, 0)))
        words = slice(g * SUBLANES, (g + 1) * SUBLANES)
        rb_ref[words, tc] = pltpu.bitcast(jnp.concatenate(rank2, axis=0).astype(BF16), U32)
        eb_ref[words, tc] = pltpu.bitcast(jnp.concatenate(eb, axis=0).astype(BF16), U32)


def _router_kernel(ht_ref, kw_ref, u_ref, v_ref, s_ref, tab_ref, rb_ref, eb_ref, up_ref, vtp_ref, v2_ref):
    s_ref[...] = jnp.dot(pltpu.bitcast(kw_ref[...], BF16), pltpu.bitcast(ht_ref[...], BF16),
                         preferred_element_type=F32)
    up_ref[...] = pltpu.bitcast(u_ref[...].astype(BF16), U32)
    vtp_ref[...] = pltpu.bitcast(v_ref[...].T.astype(BF16), U32)

    def head(hd, carry):
        row = pl.multiple_of(hd * 2 * N_KEYS, 2 * N_KEYS)
        scores = s_ref.at[pl.ds(row, 2 * N_KEYS)]
        for c in range(ROUTER_TOKENS // LANES):
            _route_select(c, scores, v2_ref.at[c], tab_ref.at[hd], rb_ref.at[hd], eb_ref.at[hd])
        return carry

    lax.fori_loop(0, PEER_HEADS, head, 0)


def _router_call(ht, kw, u_tab, v_tab, layer):
    d = D_MODEL
    tr = ROUTER_TOKENS
    nt = N_TOKENS
    n_steps = nt // tr
    rows = N_EXPERTS // n_steps
    assert rows * n_steps == N_EXPERTS and rows % BF16_ROWS == 0
    pair_spec = pl.BlockSpec((PEER_HEADS, N_KEYS // 2, tr), lambda i: (0, 0, i))
    table_spec = pl.BlockSpec((None, rows, d), lambda i: (layer, i, 0))
    return pl.pallas_call(
        _router_kernel,
        grid=(n_steps,),
        in_specs=[
            pl.BlockSpec((d // 2, tr), lambda i: (0, i)),
            pl.BlockSpec((PEER_HEADS * N_KEYS, d), lambda i: (0, 0)),
            table_spec, table_spec,
        ],
        out_specs=[
            pl.BlockSpec((PEER_HEADS * 2 * N_KEYS, tr), lambda i: (0, i)),
            pl.BlockSpec((PEER_HEADS, TAB_ROWS, tr), lambda i: (0, 0, i)),
            pair_spec, pair_spec,
            pl.BlockSpec((rows // 2, d), lambda i: (i, 0)),
            pl.BlockSpec((d // 2, rows), lambda i: (0, i)),
        ],
        out_shape=[
            jax.ShapeDtypeStruct((PEER_HEADS * 2 * N_KEYS, nt), F32),
            jax.ShapeDtypeStruct((PEER_HEADS, TAB_ROWS, nt), F32),
            jax.ShapeDtypeStruct((PEER_HEADS, N_KEYS // 2, nt), U32),
            jax.ShapeDtypeStruct((PEER_HEADS, N_KEYS // 2, nt), U32),
            jax.ShapeDtypeStruct((N_EXPERTS // 2, d), U32),
            jax.ShapeDtypeStruct((d // 2, N_EXPERTS), U32),
        ],
        scratch_shapes=[
            pltpu.VMEM((tr // LANES, PEER_TOPK, LANES), F32),
        ],
        compiler_params=pltpu.CompilerParams(
            dimension_semantics=("arbitrary",),
            vmem_limit_bytes=_vmem_limit(48 * 1024 * 1024)),
        name="peer_router",
    )(ht, kw, u_tab, v_tab)


def _gate_rows(e, act_ref, w_ref, nw_ref, eaw_ref, rb_ref, eb_ref):
    n_sub = N_KEYS // BF16_ROWS
    for c in range(EXPERT_TOKENS // LANES):
        tc = slice(c * LANES, (c + 1) * LANES)
        for k0 in range(0, n_sub, GATE_SUBS):
            subs = range(k0, k0 + GATE_SUBS)
            gates = [{kk: jnp.zeros((BF16_ROWS, LANES), BF16) for kk in subs} for _ in range(GATE_ROWS)]
            for hd in range(PEER_HEADS):
                n_rows, ea_rows = [], []
                for de in range(GATE_ROWS):
                    nw = jnp.broadcast_to(nw_ref[hd, e + de:e + de + 1, tc], (SUBLANES, LANES))
                    ew = jnp.broadcast_to(eaw_ref[hd, e + de:e + de + 1, tc], (SUBLANES, LANES))
                    n_rows.append(pltpu.bitcast(nw, BF16))
                    ea_rows.append(pltpu.bitcast(ew, BF16))
                for kk in subs:
                    ws = slice(kk * SUBLANES, (kk + 1) * SUBLANES)
                    rank2 = pltpu.bitcast(rb_ref[hd, ws, tc], BF16)
                    eb = pltpu.bitcast(eb_ref[hd, ws, tc], BF16)
                    for de in range(GATE_ROWS):
                        g = gates[de][kk]
                        gates[de][kk] = jnp.where(rank2 < n_rows[de], g + ea_rows[de] * eb, g)
            for de in range(GATE_ROWS):
                for kk in subs:
                    rows = slice((e + de) * N_KEYS + kk * BF16_ROWS, (e + de) * N_KEYS + (kk + 1) * BF16_ROWS)
                    w_ref[rows, tc] = _gelu(act_ref[rows, tc]).astype(BF16) * gates[de][kk]


def _spread_first_keys(s1_ref, tab_ref, nw_ref, eaw_ref):
    for hd in range(PEER_HEADS):
        for c in range(EXPERT_TOKENS // LANES):
            tc = slice(c * LANES, (c + 1) * LANES)
            for g in range(E1_PER_CHUNK // SUBLANES):
                rows = slice(g * SUBLANES, (g + 1) * SUBLANES)
                n1, ea = _first_key_factors(s1_ref[hd, rows, tc], tab_ref.at[hd], tc)
                nw_ref[hd, rows, tc] = _pair_word(n1)
                eaw_ref[hd, rows, tc] = _pair_word(ea)


def _expert_kernel(ht_ref, u_ref, vt_ref, s1_ref, tab_ref, rb_ref, eb_ref, x_ref, mod_ref, fg_ref,
                   out_ref, acc_ref, act0_ref, act1_ref, w0_ref, w1_ref, nw_ref, eaw_ref,
                   *, final_norm, n_steps, n_chunks):
    d = D_MODEL
    s = pl.program_id(0)
    sc = jnp.clip(s - 2, 0, n_steps - 1)
    jc = sc % n_chunks

    @pl.when(s == 0)
    def _():
        act1_ref[...] = jnp.zeros_like(act1_ref)
        w0_ref[...] = jnp.zeros_like(w0_ref)

    @pl.when(jc == 0)
    def _():
        acc_ref[...] = jnp.zeros_like(acc_ref)

    def step(act_a, act_b, w_b, w_c):
        n_groups = E1_PER_CHUNK // GATE_ROWS
        pieces_a = EXPERT_CHUNK // MXU_ROWS
        pieces_c = d // MXU_ROWS
        half = MXU_ROWS // 2
        for p in range(pieces_a):
            u = pltpu.bitcast(u_ref[p * half:(p + 1) * half, :], BF16)
            act_a[p * MXU_ROWS:(p + 1) * MXU_ROWS, :] = jnp.dot(
                u, pltpu.bitcast(ht_ref[...], BF16), preferred_element_type=F32)
            if p == 0:
                _spread_first_keys(s1_ref, tab_ref, nw_ref, eaw_ref)
            for q in range(p * n_groups // pieces_a, (p + 1) * n_groups // pieces_a):
                _gate_rows(GATE_ROWS * q, act_b, w_b, nw_ref, eaw_ref, rb_ref, eb_ref)
            if (p + 1) * pieces_c % pieces_a == 0:
                pc = (p + 1) * pieces_c // pieces_a - 1
                vt = pltpu.bitcast(vt_ref[pc * half:(pc + 1) * half, :], BF16)
                acc_ref[pc * MXU_ROWS:(pc + 1) * MXU_ROWS, :] += jnp.dot(
                    vt, w_c[...], preferred_element_type=F32)

    @pl.when(s % 2 == 0)
    def _():
        step(act0_ref, act1_ref, w1_ref, w0_ref)

    @pl.when(s % 2 == 1)
    def _():
        step(act1_ref, act0_ref, w0_ref, w1_ref)

    @pl.when((jc == n_chunks - 1) & (s >= 2))
    def _():
        mod = mod_ref[...]
        xn = x_ref[...] + mod[:, 5 * d:6 * d] * acc_ref[...].T
        if final_norm:
            ms = jnp.mean(xn * xn, axis=-1, keepdims=True)
            xn = xn * lax.rsqrt(ms + EPS) * fg_ref[...]
        out_ref[...] = xn


def _expert_call(ht, u_tab, vt_tab, scores, tab, rb, eb, x_flat, mod_l, final_g, final_norm):
    d = D_MODEL
    tb = EXPERT_TOKENS
    ec = EXPERT_CHUNK
    nt = N_TOKENS
    n_chunks = N_EXPERTS // ec
    n_steps = (nt // tb) * n_chunks

    def stage(lag):
        def split(s):
            t = jnp.clip(s - lag, 0, n_steps - 1)
            return t // n_chunks, t % n_chunks
        return split

    sa, sb, sc = stage(0), stage(1), stage(2)
    s1_spec = pl.BlockSpec((PEER_HEADS, E1_PER_CHUNK, tb), lambda s: (0, sb(s)[1], sb(s)[0]))
    tab_spec = pl.BlockSpec((PEER_HEADS, TAB_ROWS, tb), lambda s: (0, 0, sb(s)[0]))
    col_spec = pl.BlockSpec((PEER_HEADS, N_KEYS // 2, tb), lambda s: (0, 0, sb(s)[0]))
    return pl.pallas_call(
        functools.partial(_expert_kernel, final_norm=final_norm, n_steps=n_steps, n_chunks=n_chunks),
        grid=(n_steps + 2,),
        in_specs=[
            pl.BlockSpec((d // 2, tb), lambda s: (0, sa(s)[0])),
            pl.BlockSpec((ec // 2, d), lambda s: (sa(s)[1], 0)),
            pl.BlockSpec((d // 2, ec), lambda s: (0, sc(s)[1])),
            s1_spec, tab_spec, col_spec, col_spec,
            pl.BlockSpec((tb, d), lambda s: (sc(s)[0], 0)),
            pl.BlockSpec((None, 1, N_MOD * d), lambda s: (sc(s)[0] * tb // SEQ, 0, 0)),
            pl.BlockSpec((1, d), lambda s: (0, 0)),
        ],
        out_specs=pl.BlockSpec((tb, d), lambda s: (sc(s)[0], 0)),
        out_shape=jax.ShapeDtypeStruct((nt, d), F32),
        scratch_shapes=[
            pltpu.VMEM((d, tb), F32),
            pltpu.VMEM((ec, tb), F32),
            pltpu.VMEM((ec, tb), F32),
            pltpu.VMEM((ec, tb), BF16),
            pltpu.VMEM((ec, tb), BF16),
            pltpu.VMEM((PEER_HEADS, E1_PER_CHUNK, tb), U32),
            pltpu.VMEM((PEER_HEADS, E1_PER_CHUNK, tb), U32),
        ],
        compiler_params=pltpu.CompilerParams(
            dimension_semantics=("arbitrary",),
            vmem_limit_bytes=_vmem_limit(52 * 1024 * 1024)),
        name="peer_experts",
    )(ht, u_tab, vt_tab, scores.reshape(PEER_HEADS, 2 * N_KEYS, nt), tab, rb, eb, x_flat, mod_l, final_g)


def kernel(x, c, norm1_g, norm2_g, w_mod, b_mod, conv_w_in, conv_w, conv_w_out, pool_w_in, pool_w_grp,
           pool_scale, pool_w_out, peer_w_q, peer_k1, peer_k2, peer_u, peer_v, final_g):
    d = D_MODEL
    mod = _mod_call(c, w_mod, b_mod).reshape(DEPTH, BATCH, 1, N_MOD * d)
    fg = final_g.reshape(1, d)
    for i in range(DEPTH):
        g1 = norm1_g[i].reshape(1, d)
        g2 = norm2_g[i].reshape(1, d)
        jm = i // 2
        if i % 2 == 0:
            x, ht = _conv_call(x, mod[i], g1, g2, conv_w_in[jm].astype(BF16), conv_w[jm],
                               conv_w_out[jm].astype(BF16))
        else:
            x, ht = _pool_call(x, mod[i], g1, g2, pool_w_in[jm].astype(BF16), pool_w_grp[jm].astype(BF16),
                               pool_scale[jm].reshape(1, d), pool_w_out[jm].astype(BF16))
        scores, tab, rb, eb, u_pairs, vt_pairs = _router_call(
            ht, _fold_keys_call(peer_w_q, peer_k1, peer_k2, i), peer_u, peer_v, i)
        x_flat = _expert_call(ht, u_pairs, vt_pairs, scores, tab, rb, eb,
                              x.reshape(N_TOKENS, d), mod[i], fg, final_norm=(i == DEPTH - 1))
        x = x_flat.reshape(BATCH, SEQ, d)
    return x
```
